```python
import jax, jax.numpy as jnp
from jax import lax
import numpy as np

D_MODEL = 1024
BATCH = 8
SEQ = 4096
DEPTH = 2

CHUNK = 64
HEAD_DIM = 64
N_BRANCH = 4
A_HEADS = 4
A_LEFT_CHUNKS = 8
A_BAND = (A_LEFT_CHUNKS + 1) * CHUNK
REL_CLIP = 128
A_BIAS_SIZE = CHUNK + REL_CLIP
B_HEADS = 4
B_DK = 64
B_DV = 64
C_HEADS = 4
IDX_HEADS = 8
IDX_DIM = 64
TOPK_MAX = 256
Q_BLOCK = 128
D_HEADS = 4
W_LORA = 64
A_LORA = 64

ROPE_THETA = 10000.0
EPS = 1e-6
GN_EPS = 64e-5

WA = A_HEADS * HEAD_DIM
WB = B_HEADS * B_DV
WC = C_HEADS * HEAD_DIM
WD = D_HEADS * HEAD_DIM
BRANCH_W = WA

A_COLS = [WA, WA, WA, WA]
B_COLS = [B_HEADS * B_DK, B_HEADS * B_DK, WB, WB]
C_COLS = [WC, WC, WC, WC, IDX_HEADS * IDX_DIM, IDX_DIM, IDX_HEADS]
D_SHIFT_COLS = [WD, WD, WD, W_LORA, A_LORA]
COARSE_COLS = [sum(A_COLS), sum(B_COLS), sum(C_COLS), sum(D_SHIFT_COLS), WD, N_BRANCH * D_MODEL]
IN_COLS = sum(COARSE_COLS)


def _cuts(sizes):
    return [int(c) for c in np.cumsum(sizes)[:-1]]


kernel_name = 'hybrid_chunk_causal_gated_merge'


def _rmsnorm(x, g):
    xf = x.astype(jnp.float32)
    inv = lax.rsqrt(jnp.mean(xf * xf, axis=-1, keepdims=True) + EPS)
    return (xf * inv).astype(x.dtype) * g


def _heads(t, h):
    return t.reshape(t.shape[0], t.shape[1], h, -1)


def _rope(t, pos):
    half = t.shape[-1] // 2
    freqs = ROPE_THETA ** (-jnp.arange(half, dtype=jnp.float32) / half)
    ang = pos.astype(jnp.float32)[:, None] * freqs[None, :]
    cos = jnp.cos(ang)[None, :, None, :].astype(t.dtype)
    sin = jnp.sin(ang)[None, :, None, :].astype(t.dtype)
    t1, t2 = t[..., :half], t[..., half:]
    return jnp.concatenate([t1 * cos - t2 * sin, t2 * cos + t1 * sin], axis=-1)


def _chunk_band_attention(q, k, v, rel_bias):
    b_, s_, h_, dh = q.shape
    n = s_ // CHUNK
    pad = A_LEFT_CHUNKS * CHUNK
    qc = q.reshape(b_, n, CHUNK, h_, dh)
    kp = jnp.pad(k, ((0, 0), (pad, 0), (0, 0), (0, 0))).reshape(b_, n + A_LEFT_CHUNKS, CHUNK, h_, dh)
    vp = jnp.pad(v, ((0, 0), (pad, 0), (0, 0), (0, 0))).reshape(b_, n + A_LEFT_CHUNKS, CHUNK, h_, dh)
    kb = jnp.concatenate([kp[:, j:j + n] for j in range(A_LEFT_CHUNKS + 1)], axis=2)
    vb = jnp.concatenate([vp[:, j:j + n] for j in range(A_LEFT_CHUNKS + 1)], axis=2)
    s = jnp.einsum('bnqhd,bnkhd->bnhqk', qc, kb).astype(jnp.float32) * (dh ** -0.5)
    qi = jnp.arange(CHUNK)
    kj = jnp.arange(A_BAND)
    rel = jnp.clip(pad + qi[:, None] - kj[None, :], -(CHUNK - 1), REL_CLIP) + (CHUNK - 1)
    bias = rel_bias[:, rel].astype(jnp.float32)
    valid = (jnp.arange(n)[:, None] * CHUNK - pad + kj[None, :]) >= 0
    s = jnp.where(valid[None, :, None, None, :], s + bias[None, None], -jnp.inf)
    p = jax.nn.softmax(s, axis=-1).astype(v.dtype)
    o = jnp.einsum('bnhqk,bnkhd->bnqhd', p, vb)
    return o.reshape(b_, s_, h_ * dh)


def _hgrn2(q, f_logit, i, lb, norm_g):
    out_dtype = i.dtype
    b_, s_, h_, dk = q.shape
    dv = i.shape[-1]
    n = s_ // CHUNK
    lb = lb.astype(jnp.float32).reshape(h_, dk)
    fl = f_logit.astype(jnp.float32)
    log_f = jnp.logaddexp(jnp.log(lb), jnp.log1p(-lb) + jax.nn.log_sigmoid(fl))
    key = (1.0 - lb) * jax.nn.sigmoid(-fl)
    qf = jax.nn.silu(q.astype(jnp.float32))
    vf = i.astype(jnp.float32)

    def to_chunks(t):
        return t.reshape(b_, n, CHUNK, h_, t.shape[-1]).transpose(1, 0, 3, 2, 4)

    causal = jnp.tril(jnp.ones((CHUNK, CHUNK), dtype=bool))

    def step(state, xs):
        qb, kb, vb, gb = xs
        bcum = jnp.cumsum(gb, axis=2)
        diff = bcum[:, :, :, None, :] - bcum[:, :, None, :, :]
        decay = jnp.exp(jnp.where(causal[None, None, :, :, None], diff, -jnp.inf))
        att = jnp.sum(qb[:, :, :, None, :] * kb[:, :, None, :, :] * decay, axis=-1)
        o = jnp.einsum('bhts,bhsv->bhtv', att, vb) + jnp.einsum('bhtd,bhdv->bhtv', qb * jnp.exp(bcum), state)
        b_last = bcum[:, :, -1:, :]
        state = jnp.exp(b_last[:, :, 0, :])[..., None] * state + jnp.einsum('bhsd,bhsv->bhdv', kb * jnp.exp(b_last - bcum), vb)
        return state, o

    s0 = jnp.zeros((b_, h_, dk, dv), jnp.float32)
    _, o = lax.scan(step, s0, (to_chunks(qf), to_chunks(key), to_chunks(vf), to_chunks(log_f)))
    o = o.transpose(1, 0, 3, 2, 4).reshape(b_, s_, h_, dv)
    o = o * lax.rsqrt(jnp.mean(o * o, axis=-1, keepdims=True) + EPS) * norm_g.astype(jnp.float32)
    return o.reshape(b_, s_, h_ * dv).astype(out_dtype)


def _dsa(q, k, v, qi, ki, wi):
    b_, s_, h_, dh = q.shape
    topk = min(TOPK_MAX, s_ // 4)
    nb = s_ // Q_BLOCK
    key_chunk = jnp.arange(s_) // CHUNK

    def blk(xs):
        qb, qib, wib, start = xs
        qchunk = (start + jnp.arange(Q_BLOCK)) // CHUNK
        admissible = key_chunk[None, :] <= qchunk[:, None]
        sc = jax.nn.relu(jnp.einsum('bqhd,bsd->bqhs', qib, ki) * (IDX_DIM ** -0.5))
        iscore = jnp.einsum('bqhs,bqh->bqs', sc, wib * (IDX_HEADS ** -0.5)).astype(jnp.float32)
        iscore = jnp.where(admissible[None], iscore, -jnp.inf)
        _, sel = lax.top_k(iscore, topk)
        ksel = jax.vmap(lambda kk, ii: kk[ii])(k, sel)
        vsel = jax.vmap(lambda vv, ii: vv[ii])(v, sel)
        sel_ok = (sel // CHUNK) <= qchunk[None, :, None]
        s = jnp.einsum('bqhd,bqkhd->bhqk', qb, ksel).astype(jnp.float32) * (dh ** -0.5)
        s = jnp.where(sel_ok[:, None], s, -jnp.inf)
        p = jax.nn.softmax(s, axis=-1).astype(v.dtype)
        return jnp.einsum('bhqk,bqkhd->bqhd', p, vsel)

    def to_blocks(t):
        return t.reshape(b_, nb, Q_BLOCK, *t.shape[2:]).swapaxes(0, 1)

    starts = jnp.arange(nb, dtype=jnp.int32) * Q_BLOCK
    o = lax.map(blk, (to_blocks(q), to_blocks(qi), to_blocks(wi), starts))
    return o.swapaxes(0, 1).reshape(b_, s_, h_ * dh)


def _token_shift(p, mu):
    prev = jnp.pad(p, ((0, 0), (1, 0), (0, 0)))[:, :-1]
    return p + mu * (prev - p)


def _rwkv7(r, k, v, wd, ad, w0, w_up, a0, a_up, k_k, k_a, r_k, gn_g, gn_b):
    out_dtype = v.dtype
    b_, s_, _ = r.shape
    h_, n_ = D_HEADS, HEAD_DIM
    wlog = -jax.nn.softplus(-(w0 + jnp.tanh(wd) @ w_up)) - 0.5
    decay = jnp.exp(-jnp.exp(wlog.astype(jnp.float32)))
    a = jax.nn.sigmoid((a0 + ad @ a_up).astype(jnp.float32))
    kf = k.astype(jnp.float32)
    kk = (kf * k_k).reshape(b_, s_, h_, n_)
    kk = kk / jnp.maximum(jnp.sqrt(jnp.sum(kk * kk, axis=-1, keepdims=True)), 1e-12)
    kf = kf * (1.0 + (a - 1.0) * k_a)
    r4 = r.astype(jnp.float32).reshape(b_, s_, h_, n_)
    k4 = kf.reshape(b_, s_, h_, n_)
    v4 = v.astype(jnp.float32).reshape(b_, s_, h_, n_)
    w4 = decay.reshape(b_, s_, h_, n_)
    a4 = a.reshape(b_, s_, h_, n_)

    def step(state, xs):
        rt, wt, kt, vt, kkt, at = xs
        sa = jnp.einsum('bhvk,bhk->bhv', state, -kkt)
        state = state * wt[:, :, None, :] + sa[..., None] * (kkt * at)[:, :, None, :] + vt[..., None] * kt[:, :, None, :]
        return state, jnp.einsum('bhvk,bhk->bhv', state, rt)

    tm = lambda t: t.transpose(1, 0, 2, 3)
    s0 = jnp.zeros((b_, h_, n_, n_), jnp.float32)
    _, o = lax.scan(step, s0, (tm(r4), tm(w4), tm(k4), tm(v4), tm(kk), tm(a4)))
    o = tm(o)
    mean = jnp.mean(o, axis=-1, keepdims=True)
    var = jnp.mean(jnp.square(o - mean), axis=-1, keepdims=True)
    o = ((o - mean) * lax.rsqrt(var + GN_EPS)).reshape(b_, s_, h_ * n_) * gn_g + gn_b
    bonus = jnp.sum(r4 * k4 * r_k.astype(jnp.float32), axis=-1, keepdims=True) * v4
    o = o + bonus.reshape(b_, s_, h_ * n_)
    return o.astype(out_dtype)


def setup_inputs(seed: int = 0) -> dict:
    key = jax.random.key(seed)
    ks = jax.random.split(key, 24)
    f32 = jnp.float32
    nrm = lambda i, shape, scale: jax.random.normal(ks[i], shape, f32) * scale
    return {
        'x': nrm(0, (BATCH, SEQ, D_MODEL), 1.0),
        'norm_g': 1.0 + nrm(1, (DEPTH, D_MODEL), 0.02),
        'w_in': nrm(2, (DEPTH, D_MODEL, IN_COLS), D_MODEL ** -0.5),
        'a_rel_bias': nrm(3, (DEPTH, A_HEADS, A_BIAS_SIZE), 0.3),
        'b_lb_logits': nrm(4, (DEPTH, B_HEADS * B_DK), 1.0),
        'b_norm_g': 1.0 + nrm(5, (DEPTH, B_DV), 0.02),
        'd_mu': jax.random.uniform(ks[6], (DEPTH, sum(D_SHIFT_COLS)), f32),
        'd_w0': nrm(7, (DEPTH, WD), 0.5),
        'd_w_up': nrm(8, (DEPTH, W_LORA, WD), 0.5 * W_LORA ** -0.5),
        'd_a0': nrm(9, (DEPTH, WD), 0.1),
        'd_a_up': nrm(10, (DEPTH, A_LORA, WD), 0.5 * A_LORA ** -0.5),
        'd_k_k': 0.85 + nrm(11, (DEPTH, WD), 0.02),
        'd_k_a': 1.0 + nrm(12, (DEPTH, WD), 0.02),
        'd_r_k': nrm(13, (DEPTH, D_HEADS, HEAD_DIM), 0.1),
        'd_gn_g': 1.0 + nrm(14, (DEPTH, WD), 0.02),
        'd_gn_b': nrm(15, (DEPTH, WD), 0.01),
        'w_branch': nrm(16, (DEPTH, N_BRANCH, BRANCH_W, D_MODEL), BRANCH_W ** -0.5),
        'w_out': nrm(17, (DEPTH, D_MODEL, D_MODEL), D_MODEL ** -0.5),
        'final_g': 1.0 + nrm(18, (D_MODEL,), 0.02),
    }


def reference(x, norm_g, w_in, a_rel_bias, b_lb_logits, b_norm_g, d_mu, d_w0, d_w_up, d_a0, d_a_up,
              d_k_k, d_k_a, d_r_k, d_gn_g, d_gn_b, w_branch, w_out, final_g):
    pos = jnp.arange(x.shape[1], dtype=jnp.int32)
    lb_all = jnp.cumsum(jax.nn.softmax(b_lb_logits.astype(jnp.float32), axis=0), axis=0)
    lb_all = lb_all - lb_all[0:1]
    for l in range(DEPTH):
        h = _rmsnorm(x, norm_g[l])
        p = h @ w_in[l]
        pa, pb, pc, pd, dz, pg = jnp.split(p, _cuts(COARSE_COLS), axis=-1)
        aq, ak, av, az = jnp.split(pa, _cuts(A_COLS), axis=-1)
        oa = _chunk_band_attention(_heads(aq, A_HEADS), _heads(ak, A_HEADS), _heads(av, A_HEADS), a_rel_bias[l]) * jax.nn.silu(az)
        bq, bf, bi, bz = jnp.split(pb, _cuts(B_COLS), axis=-1)
        ob = _hgrn2(_heads(bq, B_HEADS), _heads(bf, B_HEADS), _heads(bi, B_HEADS), lb_all[l], b_norm_g[l]) * jax.nn.silu(bz)
        cq, ck, cv, cz, cqi, cki, cwi = jnp.split(pc, _cuts(C_COLS), axis=-1)
        oc = _dsa(_rope(_heads(cq, C_HEADS), pos), _rope(_heads(ck, C_HEADS), pos), _heads(cv, C_HEADS),
                  _rope(_heads(cqi, IDX_HEADS), pos), _rope(cki[:, :, None, :], pos)[:, :, 0, :], cwi) * jax.nn.silu(cz)
        dr, dk, dv, dwd, dad = jnp.split(_token_shift(pd, d_mu[l]), _cuts(D_SHIFT_COLS), axis=-1)
        od = _rwkv7(dr, dk, dv, dwd, dad, d_w0[l], d_w_up[l], d_a0[l], d_a_up[l], d_k_k[l], d_k_a[l],
                    d_r_k[l], d_gn_g[l], d_gn_b[l]) * jax.nn.silu(dz)
        ga, gb, gc, gd = jnp.split(pg, N_BRANCH, axis=-1)
        y = (jax.nn.sigmoid(ga) * (oa @ w_branch[l, 0]) + jax.nn.sigmoid(gb) * (ob @ w_branch[l, 1])
             + jax.nn.sigmoid(gc) * (oc @ w_branch[l, 2]) + jax.nn.sigmoid(gd) * (od @ w_branch[l, 3]))
        x = x + y @ w_out[l]
    return _rmsnorm(x, final_g)
```

```python
import functools
import math

import numpy as np
import jax
import jax.numpy as jnp
from jax import lax
from jax.experimental import pallas as pl
from jax.experimental.pallas import tpu as pltpu

F32 = jnp.float32
BF16 = jnp.bfloat16
HI = lax.Precision.HIGHEST

D_MODEL = 1024
HEADS = 4
HEAD_DIM = 64
WIDTH = HEADS * HEAD_DIM
CHUNK = 64
A_LEFT_CHUNKS = 8
REL_CLIP = 128
IDX_HEADS = 8
IDX_DIM = 64
TOPK_MAX = 256
LORA = 64
ROPE_THETA = 10000.0
EPS = 1e-6
GN_EPS = 64e-5
NEG = -1e30
INT_MIN = -2 ** 31

COL_A = 0
COL_B = 1024
COL_C = 2048
COL_CQI = 3072
COL_CKW = 3584
COL_D = 3840
COL_DWA = 4608
COL_DZ = 4864
COL_G = 5120
NP = 9216
VMEM_LIMIT = 56 * 1024 * 1024

_NT = (((1,), (1,)), ((), ()))


def _sigmoid(x):
    return 1.0 / (1.0 + jnp.exp(-x))


def _dot(a, b, precision=None):
    return jnp.dot(a, b, preferred_element_type=F32, precision=precision)


def _dot_nt(a, b, precision=None):
    return lax.dot_general(a, b, _NT, preferred_element_type=F32, precision=precision)


def _cparams(sem):
    return pltpu.CompilerParams(dimension_semantics=sem, vmem_limit_bytes=VMEM_LIMIT)


def _inproj_kernel(x_ref, g_ref, w_ref, o_ref, h_ref):
    @pl.when(pl.program_id(1) == 0)
    def _():
        x = x_ref[...]
        inv = lax.rsqrt(jnp.mean(x * x, axis=-1, keepdims=True) + EPS)
        h_ref[...] = ((x * inv) * g_ref[...]).astype(BF16)

    o_ref[...] = _dot(h_ref[...], w_ref[...])


def _inproj(x2, g, w_pad, tm, tn):
    t = x2.shape[0]
    return pl.pallas_call(
        _inproj_kernel,
        grid=(t // tm, NP // tn),
        in_specs=[
            pl.BlockSpec((tm, D_MODEL), lambda i, j: (i, 0)),
            pl.BlockSpec((1, D_MODEL), lambda i, j: (0, 0)),
            pl.BlockSpec((D_MODEL, tn), lambda i, j: (0, j)),
        ],
        out_specs=pl.BlockSpec((tm, tn), lambda i, j: (i, j)),
        out_shape=jax.ShapeDtypeStruct((t, NP), F32),
        scratch_shapes=[pltpu.VMEM((tm, D_MODEL), BF16)],
        compiler_params=_cparams(("parallel", "arbitrary")),
        name="inproj",
    )(x2, g, w_pad)


TQ_A = 256
WIN_A = 3 * TQ_A


def _band_kernel(q_ref, k0_ref, k1_ref, k2_ref, v0_ref, v1_ref, v2_ref, bias_ref, o_ref):
    i = pl.program_id(1)
    q = q_ref[...]
    kc = jnp.concatenate([k0_ref[...], k1_ref[...], k2_ref[...]], axis=0).astype(BF16)
    vc = jnp.concatenate([v0_ref[...], v1_ref[...], v2_ref[...]], axis=0).astype(BF16)
    lane = lax.broadcasted_iota(jnp.int32, (1, WIDTH), 1)
    kpos = lax.broadcasted_iota(jnp.int32, (1, WIN_A), 1) + (i - 2) * TQ_A
    kvalid = kpos >= 0
    out = jnp.zeros((TQ_A, WIDTH), F32)
    for h in range(HEADS):
        hm = (lane // HEAD_DIM) == h
        qh = jnp.where(hm, q, 0.0).astype(BF16)
        s = _dot_nt(qh, kc) * (HEAD_DIM ** -0.5) + bias_ref[h]
        s = jnp.where(kvalid, s, NEG)
        m = jnp.max(s, axis=-1, keepdims=True)
        p = jnp.exp(s - m)
        l = jnp.sum(p, axis=-1, keepdims=True)
        o = _dot(p.astype(BF16), vc) / l
        out = jnp.where(hm, o, out)
    o_ref[...] = out


def _band_attention(p, bias, b, s):
    nt = s // TQ_A
    qcol, kcol, vcol = COL_A // WIDTH, COL_A // WIDTH + 1, COL_A // WIDTH + 2

    def kv_spec(col, back):
        return pl.BlockSpec((TQ_A, WIDTH), lambda bb, i: (bb * nt + jnp.maximum(i - back, 0), col))

    return pl.pallas_call(
        _band_kernel,
        grid=(b, nt),
        in_specs=[
            pl.BlockSpec((TQ_A, WIDTH), lambda bb, i: (bb * nt + i, qcol)),
            kv_spec(kcol, 2), kv_spec(kcol, 1), kv_spec(kcol, 0),
            kv_spec(vcol, 2), kv_spec(vcol, 1), kv_spec(vcol, 0),
            pl.BlockSpec((HEADS, TQ_A, WIN_A), lambda bb, i: (0, 0, 0)),
        ],
        out_specs=pl.BlockSpec((TQ_A, WIDTH), lambda bb, i: (bb * nt + i, 0)),
        out_shape=jax.ShapeDtypeStruct((b * s, WIDTH), F32),
        compiler_params=_cparams(("parallel", "parallel")),
        name="band_attention",
    )(p, p, p, p, p, p, p, bias)


def _band_bias(rel_bias):
    ql = np.arange(TQ_A)[:, None]
    kl = np.arange(WIN_A)[None, :]
    idx = np.clip(2 * TQ_A + ql - kl, -(CHUNK - 1), REL_CLIP) + (CHUNK - 1)
    lo = CHUNK * (ql // CHUNK)
    band = (kl >= lo) & (kl < lo + (A_LEFT_CHUNKS + 1) * CHUNK)
    return jnp.where(band[None], rel_bias[:, idx].astype(F32), NEG)


TS_B = 256
_LEVELS = (1, 2, 4, 8, 16, 32)


def _hgrn_matrix():
    t = np.arange(CHUNK)
    ltri = (t[:, None] >= t[None, :]).astype(np.float32)
    mats = [ltri]
    for m in _LEVELS:
        ref = (t // (2 * m)) * (2 * m) + m - 1
        mats.append(ltri[ref])
    return np.concatenate(mats, axis=0)


def _hgrn_kernel(q_ref, f_ref, i_ref, la_ref, l1_ref, oml_ref, ng_ref, m_ref, o_ref, st_ref):
    @pl.when(pl.program_id(1) == 0)
    def _():
        st_ref[...] = jnp.zeros(st_ref.shape, F32)

    fl = f_ref[...]
    logsig = jnp.minimum(fl, 0.0) - jnp.log1p(jnp.exp(-jnp.abs(fl)))
    y = l1_ref[...] + logsig
    la = la_ref[...]
    g = jnp.maximum(la, y) + jnp.log1p(jnp.exp(-jnp.abs(la - y)))
    kk = oml_ref[...] * _sigmoid(-fl)
    qq = q_ref[...]
    qf = qq * _sigmoid(qq)
    vv = i_ref[...]
    mat = m_ref[...]
    ng = ng_ref[...]
    tt = lax.broadcasted_iota(jnp.int32, (CHUNK, CHUNK), 0)
    ss = lax.broadcasted_iota(jnp.int32, (CHUNK, CHUNK), 1)
    for c in range(TS_B // CHUNK):
        r0 = c * CHUNK
        ball = _dot(mat, g[r0:r0 + CHUNK], HI)
        for h in range(HEADS):
            c0 = h * HEAD_DIM
            bh = ball[0:CHUNK, c0:c0 + HEAD_DIM]
            qh = qf[r0:r0 + CHUNK, c0:c0 + HEAD_DIM]
            kh = kk[r0:r0 + CHUNK, c0:c0 + HEAD_DIM]
            vh = vv[r0:r0 + CHUNK, c0:c0 + HEAD_DIM]
            att = jnp.where(tt == ss, _dot_nt(qh.astype(BF16), kh.astype(BF16)), 0.0)
            for li, m in enumerate(_LEVELS):
                br = ball[(li + 1) * CHUNK:(li + 2) * CHUNK, c0:c0 + HEAD_DIM]
                ql = qh * jnp.exp(jnp.minimum(bh - br, 0.0))
                kl = kh * jnp.exp(jnp.minimum(br - bh, 0.0))
                mask = ((tt // (2 * m)) == (ss // (2 * m))) & ((tt % (2 * m)) >= m) & ((ss % (2 * m)) < m)
                att = att + jnp.where(mask, _dot_nt(ql.astype(BF16), kl.astype(BF16)), 0.0)
            st = st_ref[h]
            o = _dot(att.astype(BF16), vh.astype(BF16))
            o = o + _dot_nt((qh * jnp.exp(bh)).astype(BF16), st.astype(BF16))
            bl = bh[CHUNK - 1:CHUNK, :]
            kd = kh * jnp.exp(bl - bh)
            st_ref[h] = st * jnp.exp(bl) + _dot(vh.T.astype(BF16), kd.astype(BF16))
            o = o * lax.rsqrt(jnp.mean(o * o, axis=-1, keepdims=True) + EPS) * ng
            o_ref[r0:r0 + CHUNK, c0:c0 + HEAD_DIM] = o


def _hgrn2(p, la, l1, oml, ng, b, s):
    nt = s // TS_B
    c = COL_B // WIDTH
    row = lambda col: pl.BlockSpec((TS_B, WIDTH), lambda bb, i: (bb * nt + i, col))
    vec = lambda w: pl.BlockSpec((1, w), lambda bb, i: (0, 0))
    mat = jnp.asarray(_hgrn_matrix())
    return pl.pallas_call(
        _hgrn_kernel,
        grid=(b, nt),
        in_specs=[row(c), row(c + 1), row(c + 2), vec(WIDTH), vec(WIDTH), vec(WIDTH), vec(HEAD_DIM),
                  pl.BlockSpec(mat.shape, lambda bb, i: (0, 0))],
        out_specs=pl.BlockSpec((TS_B, WIDTH), lambda bb, i: (bb * nt + i, 0)),
        out_shape=jax.ShapeDtypeStruct((b * s, WIDTH), F32),
        scratch_shapes=[pltpu.VMEM((HEADS, HEAD_DIM, HEAD_DIM), F32)],
        compiler_params=_cparams(("parallel", "arbitrary")),
        name="hgrn2",
    )(p, p, p, la, l1, oml, ng, mat)


TM_R = 256
TQ_C = 128
KB_C = 512


def _rope(x, cos, sin_signed):
    w = x.shape[-1]
    lane = lax.broadcasted_iota(jnp.int32, x.shape, 1)
    low = (lane % HEAD_DIM) < (HEAD_DIM // 2)
    swapped = jnp.where(low, pltpu.roll(x, w - HEAD_DIM // 2, 1), pltpu.roll(x, HEAD_DIM // 2, 1))
    return x * cos + swapped * sin_signed


def _rope_kernel(q_ref, k_ref, v_ref, qi_ref, kw_ref, cos_ref, sin_ref, qo_ref, ko_ref, vo_ref, qio_ref, kio_ref):
    cos = cos_ref[...]
    sin = sin_ref[...]
    qo_ref[...] = _rope(q_ref[...], cos, sin).astype(BF16)
    ko_ref[...] = _rope(k_ref[...], cos, sin).astype(BF16)
    vo_ref[...] = v_ref[...].astype(BF16)
    cos2 = jnp.concatenate([cos, cos], axis=1)
    sin2 = jnp.concatenate([sin, sin], axis=1)
    qio_ref[...] = _rope(qi_ref[...], cos2, sin2).astype(BF16)
    kr = _rope(kw_ref[...], cos[:, :128], sin[:, :128])
    lane = lax.broadcasted_iota(jnp.int32, kr.shape, 1)
    kio_ref[...] = jnp.where(lane < IDX_DIM, kr, pltpu.roll(kr, IDX_DIM, 1)).astype(BF16)


def _rope_prep(p, cos, sin, b, s):
    t = b * s
    nt = s // TM_R
    c = COL_C // WIDTH
    row = lambda w, col: pl.BlockSpec((TM_R, w), lambda i: (i, col))
    tab = pl.BlockSpec((TM_R, WIDTH), lambda i: (i % nt, 0))
    return pl.pallas_call(
        _rope_kernel,
        grid=(t // TM_R,),
        in_specs=[row(WIDTH, c), row(WIDTH, c + 1), row(WIDTH, c + 2),
                  row(512, COL_CQI // 512), row(128, COL_CKW // 128), tab, tab],
        out_specs=[row(WIDTH, 0), row(WIDTH, 0), row(WIDTH, 0), row(512, 0), row(128, 0)],
        out_shape=[jax.ShapeDtypeStruct((t, WIDTH), BF16)] * 3
        + [jax.ShapeDtypeStruct((t, 512), BF16), jax.ShapeDtypeStruct((t, 128), BF16)],
        compiler_params=_cparams(("parallel",)),
        name="rope_prep",
    )(p, p, p, p, p, cos, sin)


def _rope_tables(s):
    half = HEAD_DIM // 2
    freqs = ROPE_THETA ** (-jnp.arange(half, dtype=F32) / half)
    ang = jnp.arange(s, dtype=jnp.int32).astype(F32)[:, None] * freqs[None, :]
    cos = jnp.cos(ang)
    sin = jnp.sin(ang)
    cos = jnp.tile(jnp.concatenate([cos, cos], axis=1), (1, HEADS))
    sin = jnp.tile(jnp.concatenate([-sin, sin], axis=1), (1, HEADS))
    return cos, sin


def _dsa_kernel(q_ref, k_ref, v_ref, qi_ref, ki_ref, w_ref, o_ref, key_ref, m_ref, l_ref, acc_ref,
                *, topk, idx_bits):
    i = pl.program_id(1)
    nkb = (i * TQ_C) // KB_C + 1
    lane128 = lax.broadcasted_iota(jnp.int32, (1, 128), 1)
    lane256 = lax.broadcasted_iota(jnp.int32, (1, WIDTH), 1)
    kiota = lax.broadcasted_iota(jnp.int32, (1, KB_C), 1)
    qchunk = (i * TQ_C + lax.broadcasted_iota(jnp.int32, (TQ_C, 1), 0)) // CHUNK
    w = w_ref[...]
    qi = qi_ref[...]

    def score_block(kb, carry):
        off = pl.multiple_of(kb * KB_C, KB_C)
        kib = ki_ref[pl.ds(off, KB_C), :]
        acc = jnp.zeros((TQ_C, KB_C), F32)
        for h in range(IDX_HEADS):
            j = h // 2
            qh = jnp.where((lane128 // IDX_DIM) == (h % 2), qi[:, 128 * j:128 * (j + 1)], jnp.zeros((), BF16))
            sc = jnp.maximum(_dot_nt(qh, kib) * (IDX_DIM ** -0.5), 0.0)
            acc = acc + sc * (w[:, IDX_DIM + h:IDX_DIM + h + 1] * (IDX_HEADS ** -0.5))
        bits = lax.bitcast_convert_type(acc, jnp.int32)
        key = bits ^ ((bits >> 31) & jnp.int32(0x7FFFFFFF))
        key = jnp.where(acc == 0.0, 0, key)
        adm = ((off + kiota) // CHUNK) <= qchunk
        key_ref[kb] = jnp.where(adm, key, INT_MIN)
        return carry

    lax.fori_loop(0, nkb, score_block, 0)

    def lane_fold(c):
        return c[:, 0:128] + c[:, 128:256] + c[:, 256:384] + c[:, 384:512]

    def count(pred):
        def body(kb, acc):
            return acc + lane_fold(jnp.where(pred(kb, key_ref[kb]), 1.0, 0.0))
        acc = lax.fori_loop(0, nkb, body, jnp.zeros((TQ_C, 128), F32))
        return jnp.sum(acc, axis=1, keepdims=True)

    def value_bit(bi, tau):
        cand = tau + lax.shift_left(jnp.int32(1), 31 - bi)
        cnt = count(lambda kb, key: key >= cand)
        return jnp.where(cnt >= topk, cand, tau)

    tau = lax.fori_loop(0, 32, value_bit, jnp.full((TQ_C, 1), INT_MIN, jnp.int32))
    need = topk - count(lambda kb, key: key > tau)

    def index_bit(bi, jmax):
        cand = jmax + lax.shift_left(jnp.int32(1), idx_bits - 1 - bi)
        cnt = count(lambda kb, key: (key == tau) & ((kb * KB_C + kiota) < cand))
        return jnp.where(cnt < need, cand, jmax)

    jmax = lax.fori_loop(0, idx_bits, index_bit, jnp.zeros((TQ_C, 1), jnp.int32))

    q = q_ref[...]
    m_ref[...] = jnp.full(m_ref.shape, NEG, F32)
    l_ref[...] = jnp.zeros(l_ref.shape, F32)
    acc_ref[...] = jnp.zeros(acc_ref.shape, F32)

    def attend_block(kb, carry):
        off = pl.multiple_of(kb * KB_C, KB_C)
        kblk = k_ref[pl.ds(off, KB_C), :]
        vblk = v_ref[pl.ds(off, KB_C), :]
        key = key_ref[kb]
        sel = (key > tau) | ((key == tau) & ((off + kiota) <= jmax) & (key > INT_MIN))
        for h in range(HEADS):
            qh = jnp.where((lane256 // HEAD_DIM) == h, q, jnp.zeros((), BF16))
            s = jnp.where(sel, _dot_nt(qh, kblk) * (HEAD_DIM ** -0.5), NEG)
            m_old = m_ref[h][:, 0:1]
            m_new = jnp.maximum(m_old, jnp.max(s, axis=-1, keepdims=True))
            p = jnp.where(sel, jnp.exp(s - m_new), 0.0)
            corr = jnp.exp(m_old - m_new)
            l_ref[h] = corr * l_ref[h] + jnp.sum(p, axis=-1, keepdims=True)
            acc_ref[h] = corr * acc_ref[h] + _dot(p.astype(BF16), vblk)
            m_ref[h] = jnp.broadcast_to(m_new, (TQ_C, 128))
        return carry

    lax.fori_loop(0, nkb, attend_block, 0)
    out = jnp.zeros((TQ_C, WIDTH), F32)
    for h in range(HEADS):
        out = jnp.where((lane256 // HEAD_DIM) == h, acc_ref[h] / l_ref[h][:, 0:1], out)
    o_ref[...] = out


def _dsa(p, qr, kr, vb, qir, kir, b, s):
    nt = s // TQ_C
    topk = min(TOPK_MAX, s // 4)
    idx_bits = int(math.log2(s)) + 1
    qrow = lambda w, col: pl.BlockSpec((TQ_C, w), lambda bb, i: (bb * nt + i, col))
    full = lambda w: pl.BlockSpec((s, w), lambda bb, i: (bb, 0))
    return pl.pallas_call(
        functools.partial(_dsa_kernel, topk=topk, idx_bits=idx_bits),
        grid=(b, nt),
        in_specs=[qrow(WIDTH, 0), full(WIDTH), full(WIDTH), qrow(512, 0), full(128),
                  qrow(128, COL_CKW // 128)],
        out_specs=qrow(WIDTH, 0),
        out_shape=jax.ShapeDtypeStruct((b * s, WIDTH), F32),
        scratch_shapes=[
            pltpu.VMEM((s // KB_C, TQ_C, KB_C), jnp.int32),
            pltpu.VMEM((HEADS, TQ_C, 128), F32),
            pltpu.VMEM((HEADS, TQ_C, 128), F32),
            pltpu.VMEM((HEADS, TQ_C, WIDTH), F32),
        ],
        compiler_params=_cparams(("parallel", "arbitrary")),
        name="dsa",
    )(qr, kr, vb, qir, kir, p)


TM_D = 256
TS_D = 256


def _dprep_kernel(r_ref, k_ref, v_ref, wa_ref, rp_ref, kp_ref, vp_ref, wap_ref,
                  mur_ref, muk_ref, muv_ref, muwa_ref, w0_ref, w2_ref, a0_ref, kkk_ref, ka_ref,
                  ro_ref, lwo_ref, kto_ref, vo_ref, kko_ref, ao_ref, *, seq):
    first = (pl.program_id(0) * TM_D) % seq == 0

    def shift(x_ref, p_ref, mu_ref):
        x = x_ref[...]
        row0 = jnp.where(first, 0.0, p_ref[7:8, :])
        rid = lax.broadcasted_iota(jnp.int32, x.shape, 0)
        prev = jnp.where(rid == 0, row0, pltpu.roll(x, 1, 0))
        return x + mu_ref[...] * (prev - x)

    r = shift(r_ref, rp_ref, mur_ref)
    k = shift(k_ref, kp_ref, muk_ref)
    v = shift(v_ref, vp_ref, muv_ref)
    wa = shift(wa_ref, wap_ref, muwa_ref)
    lane = lax.broadcasted_iota(jnp.int32, wa.shape, 1)
    lora_in = jnp.where(lane < LORA, jnp.tanh(wa), wa).astype(BF16)
    lora = _dot(lora_in, w2_ref[...])
    z = w0_ref[...] + lora[:, :WIDTH]
    wlog = -(jnp.maximum(-z, 0.0) + jnp.log1p(jnp.exp(-jnp.abs(z)))) - 0.5
    a = _sigmoid(a0_ref[...] + lora[:, WIDTH:])
    ro_ref[...] = r
    lwo_ref[...] = -jnp.exp(wlog)
    kto_ref[...] = k * (1.0 + (a - 1.0) * ka_ref[...])
    vo_ref[...] = v
    kko_ref[...] = k * kkk_ref[...]
    ao_ref[...] = a


def _dprep(p, mu, w0, w2, a0, kkk, ka, b, s):
    t = b * s
    rb = TM_D // 8
    c = COL_D // WIDTH
    row = lambda w, col: pl.BlockSpec((TM_D, w), lambda i: (i, col))
    prev = lambda w, col: pl.BlockSpec((8, w), lambda i: (jnp.maximum(i * rb - 1, 0), col))
    vec = lambda w: pl.BlockSpec((1, w), lambda i: (0, 0))
    mur, muk, muv = mu[:, 0:256], mu[:, 256:512], mu[:, 512:768]
    muwa = mu[:, 768:896]
    return pl.pallas_call(
        functools.partial(_dprep_kernel, seq=s),
        grid=(t // TM_D,),
        in_specs=[row(WIDTH, c), row(WIDTH, c + 1), row(WIDTH, c + 2), row(128, COL_DWA // 128),
                  prev(WIDTH, c), prev(WIDTH, c + 1), prev(WIDTH, c + 2), prev(128, COL_DWA // 128),
                  vec(WIDTH), vec(WIDTH), vec(WIDTH), vec(128), vec(WIDTH),
                  pl.BlockSpec((128, 2 * WIDTH), lambda i: (0, 0)), vec(WIDTH), vec(WIDTH), vec(WIDTH)],
        out_specs=[row(WIDTH, 0)] * 6,
        out_shape=[jax.ShapeDtypeStruct((t, WIDTH), F32)] * 6,
        compiler_params=_cparams(("parallel",)),
        name="rwkv_prep",
    )(p, p, p, p, p, p, p, p, mur, muk, muv, muwa, w0, w2, a0, kkk, ka)


def _rwkv_kernel(r_ref, lw_ref, kt_ref, v_ref, kk_ref, a_ref, rk_ref, gg_ref, gb_ref, o_ref, st_ref):
    @pl.when(pl.program_id(1) == 0)
    def _():
        st_ref[...] = jnp.zeros(st_ref.shape, F32)

    tt = lax.broadcasted_iota(jnp.int32, (CHUNK, CHUNK), 0)
    ss = lax.broadcasted_iota(jnp.int32, (CHUNK, CHUNK), 1)
    strict = tt > ss
    incl = tt >= ss
    ltri = jnp.where(incl, 1.0, 0.0)
    eye = jnp.where(tt == ss, 1.0, 0.0)
    for c in range(TS_D // CHUNK):
        r0 = c * CHUNK
        for h in range(HEADS):
            c0 = h * HEAD_DIM
            sl = (slice(r0, r0 + CHUNK), slice(c0, c0 + HEAD_DIM))
            rr, lw, kt, vv, kkr, aa = (x[sl] for x in (r_ref, lw_ref, kt_ref, v_ref, kk_ref, a_ref))
            kk = kkr / jnp.maximum(jnp.sqrt(jnp.sum(kkr * kkr, axis=-1, keepdims=True)), 1e-12)
            be = kk * aa
            b = _dot(ltri, lw, HI)
            bl = b[CHUNK - 1:CHUNK, :]
            eb = jnp.exp(b)
            enb = jnp.exp(-b)
            qa = -kk * jnp.exp(b - lw)
            qr = rr * eb
            kb = be * enb
            kkt = kt * enb
            a_ab = jnp.where(strict, _dot_nt(qa, kb, HI), 0.0)
            a_ak = jnp.where(strict, _dot_nt(qa, kkt, HI), 0.0)
            a_rb = jnp.where(incl, _dot_nt(qr, kb, HI), 0.0)
            a_rk = jnp.where(incl, _dot_nt(qr, kkt, HI), 0.0)
            tm = eye + a_ab
            pw = a_ab
            for _ in range(int(math.log2(CHUNK)) - 1):
                pw = _dot(pw, pw, HI)
                tm = tm + _dot(tm, pw, HI)
            st = st_ref[h]
            sa = _dot(tm, _dot_nt(qa, st, HI) + _dot(a_ak, vv, HI), HI)
            o = _dot_nt(qr, st, HI) + _dot(a_rb, sa, HI) + _dot(a_rk, vv, HI)
            dec = jnp.exp(bl - b)
            st_ref[h] = st * jnp.exp(bl) + _dot(sa.T, be * dec, HI) + _dot(vv.T, kt * dec, HI)
            mean = jnp.mean(o, axis=-1, keepdims=True)
            var = jnp.mean(jnp.square(o - mean), axis=-1, keepdims=True)
            o = (o - mean) * lax.rsqrt(var + GN_EPS) * gg_ref[:, c0:c0 + HEAD_DIM] + gb_ref[:, c0:c0 + HEAD_DIM]
            bonus = jnp.sum(rr * kt * rk_ref[:, c0:c0 + HEAD_DIM], axis=-1, keepdims=True) * vv
            o_ref[sl] = o + bonus


def _rwkv7(r, lw, kt, v, kk, a, rk, gg, gb, b, s):
    nt = s // TS_D
    row = pl.BlockSpec((TS_D, WIDTH), lambda bb, i: (bb * nt + i, 0))
    vec = pl.BlockSpec((1, WIDTH), lambda bb, i: (0, 0))
    return pl.pallas_call(
        _rwkv_kernel,
        grid=(b, nt),
        in_specs=[row] * 6 + [vec] * 3,
        out_specs=row,
        out_shape=jax.ShapeDtypeStruct((b * s, WIDTH), F32),
        scratch_shapes=[pltpu.VMEM((HEADS, HEAD_DIM, HEAD_DIM), F32)],
        compiler_params=_cparams(("parallel", "arbitrary")),
        name="rwkv7",
    )(r, lw, kt, v, kk, a, rk, gg, gb)


TM_M = 256


def _merge_kernel(oa_ref, ob_ref, oc_ref, od_ref, za_ref, zb_ref, zc_ref, zd_ref,
                  ga_ref, gb_ref, gc_ref, gd_ref, x_ref, wb_ref, wo_ref, fg_ref, o_ref, *, final):
    def branch(o_r, z_r, g_r, idx):
        z = z_r[...]
        u = (o_r[...] * (z * _sigmoid(z))).astype(BF16)
        return _sigmoid(g_r[...]) * _dot(u, wb_ref[idx])

    y = (branch(oa_ref, za_ref, ga_ref, 0) + branch(ob_ref, zb_ref, gb_ref, 1)
         + branch(oc_ref, zc_ref, gc_ref, 2) + branch(od_ref, zd_ref, gd_ref, 3))
    xn = x_ref[...] + _dot(y.astype(BF16), wo_ref[...])
    if final:
        inv = lax.rsqrt(jnp.mean(xn * xn, axis=-1, keepdims=True) + EPS)
        xn = (xn * inv) * fg_ref[...]
    o_ref[...] = xn


def _merge(oa, ob, oc, od, p, x2, wb, wo, fg, final):
    t = x2.shape[0]
    row = lambda w, col: pl.BlockSpec((TM_M, w), lambda i: (i, col))
    zcol = lambda base: (base + 3 * WIDTH) // WIDTH
    gcol = lambda n: COL_G // D_MODEL + n
    return pl.pallas_call(
        functools.partial(_merge_kernel, final=final),
        grid=(t // TM_M,),
        in_specs=[row(WIDTH, 0)] * 4
        + [row(WIDTH, zcol(COL_A)), row(WIDTH, zcol(COL_B)), row(WIDTH, zcol(COL_C)), row(WIDTH, COL_DZ // WIDTH)]
        + [row(D_MODEL, gcol(n)) for n in range(4)]
        + [row(D_MODEL, 0),
           pl.BlockSpec((4, WIDTH, D_MODEL), lambda i: (0, 0, 0)),
           pl.BlockSpec((D_MODEL, D_MODEL), lambda i: (0, 0)),
           pl.BlockSpec((1, D_MODEL), lambda i: (0, 0))],
        out_specs=row(D_MODEL, 0),
        out_shape=jax.ShapeDtypeStruct((t, D_MODEL), F32),
        compiler_params=_cparams(("parallel",)),
        name="merge",
    )(oa, ob, oc, od, p, p, p, p, p, p, p, p, x2, wb, wo, fg)


def _pad_w_in(w):
    def seg(lo, hi, width):
        s = w[:, lo:hi]
        return jnp.pad(s, ((0, 0), (0, width - (hi - lo))))
    c_end = 2048 + 1608
    d_end = c_end + 896
    parts = [w[:, :2048], seg(2048, c_end, COL_D - COL_C), seg(c_end, d_end, COL_DZ - COL_D), w[:, d_end:]]
    return jnp.concatenate(parts, axis=1).astype(BF16)


def kernel(x, norm_g, w_in, a_rel_bias, b_lb_logits, b_norm_g, d_mu, d_w0, d_w_up, d_a0, d_a_up,
           d_k_k, d_k_a, d_r_k, d_gn_g, d_gn_b, w_branch, w_out, final_g):
    b, s, d = x.shape
    depth = norm_g.shape[0]
    t = b * s
    x2 = x.reshape(t, d)
    cos, sin = _rope_tables(s)
    lb_all = jnp.cumsum(jax.nn.softmax(b_lb_logits.astype(F32), axis=0), axis=0)
    lb_all = lb_all - lb_all[0:1]
    tm_in = 512 if t % 512 == 0 else 256
    for l in range(depth):
        p = _inproj(x2, norm_g[l][None, :], _pad_w_in(w_in[l]), tm_in, 1024)
        oa = _band_attention(p, _band_bias(a_rel_bias[l]), b, s)
        lb = lb_all[l][None, :]
        ob = _hgrn2(p, jnp.log(lb), jnp.log1p(-lb), 1.0 - lb, b_norm_g[l][None, :], b, s)
        qr, kr, vb, qir, kir = _rope_prep(p, cos, sin, b, s)
        oc = _dsa(p, qr, kr, vb, qir, kir, b, s)
        zeros = jnp.zeros((LORA, WIDTH), F32)
        w2 = jnp.concatenate([jnp.concatenate([d_w_up[l], zeros], axis=1),
                              jnp.concatenate([zeros, d_a_up[l]], axis=1)], axis=0).astype(BF16)
        r, lw, kt, v, kk, a = _dprep(p, d_mu[l][None, :], d_w0[l][None, :], w2, d_a0[l][None, :],
                                     d_k_k[l][None, :], d_k_a[l][None, :], b, s)
        od = _rwkv7(r, lw, kt, v, kk, a, d_r_k[l].reshape(1, WIDTH), d_gn_g[l][None, :], d_gn_b[l][None, :], b, s)
        x2 = _merge(oa, ob, oc, od, p, x2, w_branch[l].astype(BF16), w_out[l].astype(BF16),
                    final_g[None, :], l == depth - 1)
    return x2.reshape(b, s, d)
```

```python
import functools
import math

import numpy as np
import jax
import jax.numpy as jnp
from jax import lax
from jax.experimental import pallas as pl
from jax.experimental.pallas import tpu as pltpu

F32 = jnp.float32
BF16 = jnp.bfloat16
HI = lax.Precision.HIGHEST

D_MODEL = 1024
HEADS = 4
HEAD_DIM = 64
WIDTH = HEADS * HEAD_DIM
CHUNK = 64
A_LEFT_CHUNKS = 8
REL_CLIP = 128
IDX_HEADS = 8
IDX_DIM = 64
TOPK_MAX = 256
LORA = 64
ROPE_THETA = 10000.0
EPS = 1e-6
GN_EPS = 64e-5
NEG = -1e30
INT_MIN = -2 ** 31

COL_A = 0
COL_B = 1024
COL_C = 2048
COL_CQI = 3072
COL_CKW = 3584
COL_D = 3840
COL_DWA = 4608
COL_DZ = 4864
COL_G = 5120
NP = 9216
VMEM_LIMIT = 56 * 1024 * 1024

_NT = (((1,), (1,)), ((), ()))


def _sigmoid(x):
    return 1.0 / (1.0 + jnp.exp(-x))


def _dot(a, b, precision=None):
    return jnp.dot(a, b, preferred_element_type=F32, precision=precision)


def _dot_nt(a, b, precision=None):
    return lax.dot_general(a, b, _NT, preferred_element_type=F32, precision=precision)


def _dot3(a, b, left):
    x = a if left else b
    hi = x.astype(BF16)
    rest = x - hi.astype(F32)
    mid = rest.astype(BF16)
    lo = (rest - mid.astype(F32)).astype(BF16)
    if left:
        return _dot(hi, b) + _dot(mid, b) + _dot(lo, b)
    return _dot(a, hi) + _dot(a, mid) + _dot(a, lo)


def _cparams(sem):
    return pltpu.CompilerParams(dimension_semantics=sem, vmem_limit_bytes=VMEM_LIMIT)


def _inproj_kernel(x_ref, g_ref, w_ref, o_ref, h_ref):
    @pl.when(pl.program_id(1) == 0)
    def _():
        x = x_ref[...]
        inv = lax.rsqrt(jnp.mean(x * x, axis=-1, keepdims=True) + EPS)
        h_ref[...] = ((x * inv) * g_ref[...]).astype(BF16)

    o_ref[...] = _dot(h_ref[...], w_ref[...])


def _inproj(x2, g, w_pad, tm, tn):
    t = x2.shape[0]
    return pl.pallas_call(
        _inproj_kernel,
        grid=(t // tm, NP // tn),
        in_specs=[
            pl.BlockSpec((tm, D_MODEL), lambda i, j: (i, 0)),
            pl.BlockSpec((1, D_MODEL), lambda i, j: (0, 0)),
            pl.BlockSpec((D_MODEL, tn), lambda i, j: (0, j)),
        ],
        out_specs=pl.BlockSpec((tm, tn), lambda i, j: (i, j)),
        out_shape=jax.ShapeDtypeStruct((t, NP), F32),
        scratch_shapes=[pltpu.VMEM((tm, D_MODEL), BF16)],
        compiler_params=_cparams(("parallel", "arbitrary")),
        name="inproj",
    )(x2, g, w_pad)


TQ_A = 256
WIN_A = 3 * TQ_A


def _band_kernel(q_ref, k0_ref, k1_ref, k2_ref, v0_ref, v1_ref, v2_ref, bias_ref, o_ref):
    i = pl.program_id(1)
    q = q_ref[...]
    kc = jnp.concatenate([k0_ref[...], k1_ref[...], k2_ref[...]], axis=0).astype(BF16)
    vc = jnp.concatenate([v0_ref[...], v1_ref[...], v2_ref[...]], axis=0).astype(BF16)
    lane = lax.broadcasted_iota(jnp.int32, (1, WIDTH), 1)
    kpos = lax.broadcasted_iota(jnp.int32, (1, WIN_A), 1) + (i - 2) * TQ_A
    kvalid = kpos >= 0
    out = jnp.zeros((TQ_A, WIDTH), F32)
    for h in range(HEADS):
        hm = (lane // HEAD_DIM) == h
        qh = jnp.where(hm, q, 0.0).astype(BF16)
        s = _dot_nt(qh, kc) * (HEAD_DIM ** -0.5) + bias_ref[h]
        s = jnp.where(kvalid, s, NEG)
        m = jnp.max(s, axis=-1, keepdims=True)
        p = jnp.exp(s - m)
        l = jnp.sum(p, axis=-1, keepdims=True)
        o = _dot(p.astype(BF16), vc) / l
        out = jnp.where(hm, o, out)
    o_ref[...] = out


def _band_attention(p, bias, b, s):
    nt = s // TQ_A
    qcol, kcol, vcol = COL_A // WIDTH, COL_A // WIDTH + 1, COL_A // WIDTH + 2

    def kv_spec(col, back):
        return pl.BlockSpec((TQ_A, WIDTH), lambda bb, i: (bb * nt + jnp.maximum(i - back, 0), col))

    return pl.pallas_call(
        _band_kernel,
        grid=(b, nt),
        in_specs=[
            pl.BlockSpec((TQ_A, WIDTH), lambda bb, i: (bb * nt + i, qcol)),
            kv_spec(kcol, 2), kv_spec(kcol, 1), kv_spec(kcol, 0),
            kv_spec(vcol, 2), kv_spec(vcol, 1), kv_spec(vcol, 0),
            pl.BlockSpec((HEADS, TQ_A, WIN_A), lambda bb, i: (0, 0, 0)),
        ],
        out_specs=pl.BlockSpec((TQ_A, WIDTH), lambda bb, i: (bb * nt + i, 0)),
        out_shape=jax.ShapeDtypeStruct((b * s, WIDTH), F32),
        compiler_params=_cparams(("parallel", "parallel")),
        name="band_attention",
    )(p, p, p, p, p, p, p, bias)


def _band_bias(rel_bias):
    tab = rel_bias.astype(F32)
    n = tab.shape[1]
    d_lo, d_hi = 2 * TQ_A - (WIN_A - 1), 2 * TQ_A + TQ_A - 1
    n_lo = -(CHUNK - 1) - d_lo + 1
    n_hi = d_hi - REL_CLIP + 1
    v = jnp.concatenate([jnp.repeat(tab[:, :1], n_lo, axis=1), tab[:, 1:n - 1],
                         jnp.repeat(tab[:, n - 1:], n_hi, axis=1)], axis=1)
    span = WIN_A + TQ_A
    u = jnp.concatenate([v[:, :WIN_A][:, ::-1], jnp.zeros((HEADS, span - v.shape[1]), F32),
                         v[:, WIN_A:][:, ::-1]], axis=1)
    skew = jnp.tile(u, (1, TQ_A))[:, :TQ_A * (span - 1)].reshape(HEADS, TQ_A, span - 1)[:, :, :WIN_A]
    ql = np.arange(TQ_A)[:, None]
    kl = np.arange(WIN_A)[None, :]
    lo = CHUNK * (ql // CHUNK)
    band = (kl >= lo) & (kl < lo + (A_LEFT_CHUNKS + 1) * CHUNK)
    return jnp.where(band[None], skew, NEG)


TS_B = 256
_LEVELS = (1, 2, 4, 8, 16, 32)


def _hgrn_matrix():
    t = np.arange(CHUNK)
    ltri = (t[:, None] >= t[None, :]).astype(np.float32)
    mats = [ltri]
    for m in _LEVELS:
        ref = (t // (2 * m)) * (2 * m) + m - 1
        mats.append(ltri[ref])
    return np.concatenate(mats, axis=0)


def _hgrn_kernel(q_ref, f_ref, i_ref, la_ref, l1_ref, oml_ref, ng_ref, m_ref, o_ref, st_ref):
    @pl.when(pl.program_id(1) == 0)
    def _():
        st_ref[...] = jnp.zeros(st_ref.shape, F32)

    fl = f_ref[...]
    logsig = jnp.minimum(fl, 0.0) - jnp.log1p(jnp.exp(-jnp.abs(fl)))
    y = l1_ref[...] + logsig
    la = la_ref[...]
    g = jnp.maximum(la, y) + jnp.log1p(jnp.exp(-jnp.abs(la - y)))
    kk = oml_ref[...] * _sigmoid(-fl)
    qq = q_ref[...]
    qf = qq * _sigmoid(qq)
    vv = i_ref[...]
    mat = m_ref[...]
    ng = ng_ref[...]
    tt = lax.broadcasted_iota(jnp.int32, (CHUNK, CHUNK), 0)
    ss = lax.broadcasted_iota(jnp.int32, (CHUNK, CHUNK), 1)
    for c in range(TS_B // CHUNK):
        r0 = c * CHUNK
        ball = _dot(mat, g[r0:r0 + CHUNK], HI)
        for h in range(HEADS):
            c0 = h * HEAD_DIM
            bh = ball[0:CHUNK, c0:c0 + HEAD_DIM]
            qh = qf[r0:r0 + CHUNK, c0:c0 + HEAD_DIM]
            kh = kk[r0:r0 + CHUNK, c0:c0 + HEAD_DIM]
            vh = vv[r0:r0 + CHUNK, c0:c0 + HEAD_DIM]
            att = jnp.where(tt == ss, _dot_nt(qh.astype(BF16), kh.astype(BF16)), 0.0)
            for li, m in enumerate(_LEVELS):
                br = ball[(li + 1) * CHUNK:(li + 2) * CHUNK, c0:c0 + HEAD_DIM]
                ql = qh * jnp.exp(jnp.minimum(bh - br, 0.0))
                kl = kh * jnp.exp(jnp.minimum(br - bh, 0.0))
                mask = ((tt // (2 * m)) == (ss // (2 * m))) & ((tt % (2 * m)) >= m) & ((ss % (2 * m)) < m)
                att = att + jnp.where(mask, _dot_nt(ql.astype(BF16), kl.astype(BF16)), 0.0)
            st = st_ref[h]
            o = _dot(att.astype(BF16), vh.astype(BF16))
            o = o + _dot_nt((qh * jnp.exp(bh)).astype(BF16), st.astype(BF16))
            bl = bh[CHUNK - 1:CHUNK, :]
            kd = kh * jnp.exp(bl - bh)
            st_ref[h] = st * jnp.exp(bl) + _dot(vh.T.astype(BF16), kd.astype(BF16))
            o = o * lax.rsqrt(jnp.mean(o * o, axis=-1, keepdims=True) + EPS) * ng
            o_ref[r0:r0 + CHUNK, c0:c0 + HEAD_DIM] = o


def _hgrn2(p, la, l1, oml, ng, b, s):
    nt = s // TS_B
    c = COL_B // WIDTH
    row = lambda col: pl.BlockSpec((TS_B, WIDTH), lambda bb, i: (bb * nt + i, col))
    vec = lambda w: pl.BlockSpec((1, w), lambda bb, i: (0, 0))
    mat = jnp.asarray(_hgrn_matrix())
    return pl.pallas_call(
        _hgrn_kernel,
        grid=(b, nt),
        in_specs=[row(c), row(c + 1), row(c + 2), vec(WIDTH), vec(WIDTH), vec(WIDTH), vec(HEAD_DIM),
                  pl.BlockSpec(mat.shape, lambda bb, i: (0, 0))],
        out_specs=pl.BlockSpec((TS_B, WIDTH), lambda bb, i: (bb * nt + i, 0)),
        out_shape=jax.ShapeDtypeStruct((b * s, WIDTH), F32),
        scratch_shapes=[pltpu.VMEM((HEADS, HEAD_DIM, HEAD_DIM), F32)],
        compiler_params=_cparams(("parallel", "arbitrary")),
        name="hgrn2",
    )(p, p, p, la, l1, oml, ng, mat)


TM_R = 256
TQ_C = 128
KB_C = 512


def _rope(x, cos, sin_signed):
    w = x.shape[-1]
    lane = lax.broadcasted_iota(jnp.int32, x.shape, 1)
    low = (lane % HEAD_DIM) < (HEAD_DIM // 2)
    swapped = jnp.where(low, pltpu.roll(x, w - HEAD_DIM // 2, 1), pltpu.roll(x, HEAD_DIM // 2, 1))
    return x * cos + swapped * sin_signed


def _rope_kernel(q_ref, k_ref, v_ref, qi_ref, kw_ref, cos_ref, sin_ref, qo_ref, ko_ref, vo_ref, qio_ref, kio_ref):
    cos = cos_ref[...]
    sin = sin_ref[...]
    qo_ref[...] = _rope(q_ref[...], cos, sin).astype(BF16)
    ko_ref[...] = _rope(k_ref[...], cos, sin).astype(BF16)
    vo_ref[...] = v_ref[...].astype(BF16)
    cos2 = jnp.concatenate([cos, cos], axis=1)
    sin2 = jnp.concatenate([sin, sin], axis=1)
    qio_ref[...] = _rope(qi_ref[...], cos2, sin2).astype(BF16)
    kr = _rope(kw_ref[...], cos[:, :128], sin[:, :128])
    lane = lax.broadcasted_iota(jnp.int32, kr.shape, 1)
    kio_ref[...] = jnp.where(lane < IDX_DIM, kr, pltpu.roll(kr, IDX_DIM, 1)).astype(BF16)


def _rope_prep(p, cos, sin, b, s):
    t = b * s
    nt = s // TM_R
    c = COL_C // WIDTH
    row = lambda w, col: pl.BlockSpec((TM_R, w), lambda i: (i, col))
    tab = pl.BlockSpec((TM_R, WIDTH), lambda i: (i % nt, 0))
    return pl.pallas_call(
        _rope_kernel,
        grid=(t // TM_R,),
        in_specs=[row(WIDTH, c), row(WIDTH, c + 1), row(WIDTH, c + 2),
                  row(512, COL_CQI // 512), row(128, COL_CKW // 128), tab, tab],
        out_specs=[row(WIDTH, 0), row(WIDTH, 0), row(WIDTH, 0), row(512, 0), row(128, 0)],
        out_shape=[jax.ShapeDtypeStruct((t, WIDTH), BF16)] * 3
        + [jax.ShapeDtypeStruct((t, 512), BF16), jax.ShapeDtypeStruct((t, 128), BF16)],
        compiler_params=_cparams(("parallel",)),
        name="rope_prep",
    )(p, p, p, p, p, cos, sin)


def _rope_tables(s):
    half = HEAD_DIM // 2
    freqs = ROPE_THETA ** (-jnp.arange(half, dtype=F32) / half)
    ang = jnp.arange(s, dtype=jnp.int32).astype(F32)[:, None] * freqs[None, :]
    cos = jnp.cos(ang)
    sin = jnp.sin(ang)
    cos = jnp.tile(jnp.concatenate([cos, cos], axis=1), (1, HEADS))
    sin = jnp.tile(jnp.concatenate([-sin, sin], axis=1), (1, HEADS))
    return cos, sin


def _dsa_kernel(q_ref, k_ref, v_ref, qi_ref, ki_ref, w_ref, o_ref, key_ref, m_ref, l_ref, acc_ref,
                *, topk, idx_bits):
    i = pl.program_id(1)
    nkb = (i * TQ_C) // KB_C + 1
    lane128 = lax.broadcasted_iota(jnp.int32, (1, 128), 1)
    lane256 = lax.broadcasted_iota(jnp.int32, (1, WIDTH), 1)
    kiota = lax.broadcasted_iota(jnp.int32, (1, KB_C), 1)
    qchunk = (i * TQ_C + lax.broadcasted_iota(jnp.int32, (TQ_C, 1), 0)) // CHUNK
    w = w_ref[...] * (IDX_HEADS ** -0.5)
    qi = qi_ref[...] * jnp.asarray(IDX_DIM ** -0.5, BF16)
    qi_heads = [jnp.where((lane128 // IDX_DIM) == (h % 2), qi[:, 128 * (h // 2):128 * (h // 2 + 1)],
                          jnp.zeros((), BF16)) for h in range(IDX_HEADS)]

    def score_block(kb, carry):
        off = pl.multiple_of(kb * KB_C, KB_C)
        kib = ki_ref[pl.ds(off, KB_C), :]
        acc = jnp.zeros((TQ_C, KB_C), F32)
        for h in range(IDX_HEADS):
            acc = acc + jnp.maximum(_dot_nt(qi_heads[h], kib), 0.0) * w[:, IDX_DIM + h:IDX_DIM + h + 1]
        bits = lax.bitcast_convert_type(acc, jnp.int32)
        key = bits ^ ((bits >> 31) & jnp.int32(0x7FFFFFFF))
        key = jnp.where(acc == 0.0, 0, key)
        adm = ((off + kiota) // CHUNK) <= qchunk
        key_ref[kb] = jnp.where(adm, key, INT_MIN)
        return carry

    lax.fori_loop(0, nkb, score_block, 0)

    def lane_fold(c):
        return c[:, 0:128] + c[:, 128:256] + c[:, 256:384] + c[:, 384:512]

    def count(pred):
        def body(kb, acc):
            return acc + lane_fold(jnp.where(pred(kb, key_ref[kb]), 1.0, 0.0))
        acc = lax.fori_loop(0, nkb, body, jnp.zeros((TQ_C, 128), F32))
        return jnp.sum(acc, axis=1, keepdims=True)

    def value_bit(bi, tau):
        cand = tau + lax.shift_left(jnp.int32(1), 31 - bi)
        cnt = count(lambda kb, key: key >= cand)
        return jnp.where(cnt >= topk, cand, tau)

    tau = lax.fori_loop(0, 32, value_bit, jnp.full((TQ_C, 1), INT_MIN, jnp.int32))
    need = topk - count(lambda kb, key: key > tau)
    ties = count(lambda kb, key: key == tau)

    def index_search():
        def index_bit(bi, jmax):
            cand = jmax + lax.shift_left(jnp.int32(1), idx_bits - 1 - bi)
            cnt = count(lambda kb, key: (key == tau) & ((kb * KB_C + kiota) < cand))
            return jnp.where(cnt < need, cand, jmax)
        return lax.fori_loop(0, idx_bits, index_bit, jnp.zeros((TQ_C, 1), jnp.int32))

    excess = jnp.max(jnp.where((ties > need) & (tau > INT_MIN), 1.0, 0.0)) > 0.5
    jmax = lax.cond(excess, index_search, lambda: jnp.full((TQ_C, 1), 2 ** (idx_bits - 1), jnp.int32))

    q = q_ref[...] * jnp.asarray(HEAD_DIM ** -0.5, BF16)
    q4 = jnp.concatenate([jnp.where((lane256 // HEAD_DIM) == h, q, jnp.zeros((), BF16)) for h in range(HEADS)],
                         axis=0)
    m_ref[...] = jnp.full(m_ref.shape, NEG, F32)
    l_ref[...] = jnp.zeros(l_ref.shape, F32)
    acc_ref[...] = jnp.zeros(acc_ref.shape, F32)

    def attend_block(kb, carry):
        off = pl.multiple_of(kb * KB_C, KB_C)
        kblk = k_ref[pl.ds(off, KB_C), :]
        vblk = v_ref[pl.ds(off, KB_C), :]
        key = key_ref[kb]
        sel = (key > tau) | ((key == tau) & ((off + kiota) <= jmax) & (key > INT_MIN))
        madd = jnp.where(sel, 0.0, NEG)
        s = _dot_nt(q4, kblk).reshape(HEADS, TQ_C, KB_C) + madd[None]
        m_old = m_ref[...][:, :, 0:1]
        m_new = jnp.maximum(m_old, jnp.max(s, axis=-1, keepdims=True))
        p = jnp.exp(s - m_new)
        corr = jnp.exp(m_old - m_new)
        l_ref[...] = corr * l_ref[...] + jnp.sum(p, axis=-1, keepdims=True)
        pv = _dot(p.reshape(HEADS * TQ_C, KB_C).astype(BF16), vblk).reshape(HEADS, TQ_C, WIDTH)
        acc_ref[...] = corr * acc_ref[...] + pv
        m_ref[...] = jnp.broadcast_to(m_new, (HEADS, TQ_C, 128))
        return carry

    lax.fori_loop(0, nkb, attend_block, 0)
    out = jnp.zeros((TQ_C, WIDTH), F32)
    for h in range(HEADS):
        out = jnp.where((lane256 // HEAD_DIM) == h, acc_ref[h] / l_ref[h][:, 0:1], out)
    o_ref[...] = out


def _dsa(p, qr, kr, vb, qir, kir, b, s):
    nt = s // TQ_C
    topk = min(TOPK_MAX, s // 4)
    idx_bits = int(math.log2(s)) + 1
    qrow = lambda w, col: pl.BlockSpec((TQ_C, w), lambda bb, i: (bb * nt + i, col))
    full = lambda w: pl.BlockSpec((s, w), lambda bb, i: (bb, 0))
    return pl.pallas_call(
        functools.partial(_dsa_kernel, topk=topk, idx_bits=idx_bits),
        grid=(b, nt),
        in_specs=[qrow(WIDTH, 0), full(WIDTH), full(WIDTH), qrow(512, 0), full(128),
                  qrow(128, COL_CKW // 128)],
        out_specs=qrow(WIDTH, 0),
        out_shape=jax.ShapeDtypeStruct((b * s, WIDTH), F32),
        scratch_shapes=[
            pltpu.VMEM((s // KB_C, TQ_C, KB_C), jnp.int32),
            pltpu.VMEM((HEADS, TQ_C, 128), F32),
            pltpu.VMEM((HEADS, TQ_C, 128), F32),
            pltpu.VMEM((HEADS, TQ_C, WIDTH), F32),
        ],
        compiler_params=_cparams(("parallel", "arbitrary")),
        name="dsa",
    )(qr, kr, vb, qir, kir, p)


TM_D = 256
TS_D = 256
assert TS_D == WIDTH and CHUNK == HEAD_DIM


def _dprep_kernel(r_ref, k_ref, v_ref, wa_ref, rp_ref, kp_ref, vp_ref, wap_ref,
                  mur_ref, muk_ref, muv_ref, muwa_ref, w0_ref, w2_ref, a0_ref, kkk_ref, ka_ref,
                  ro_ref, lwo_ref, kto_ref, vo_ref, kko_ref, ao_ref, *, seq):
    first = (pl.program_id(0) * TM_D) % seq == 0

    def shift(x_ref, p_ref, mu_ref):
        x = x_ref[...]
        row0 = jnp.where(first, 0.0, p_ref[7:8, :])
        rid = lax.broadcasted_iota(jnp.int32, x.shape, 0)
        prev = jnp.where(rid == 0, row0, pltpu.roll(x, 1, 0))
        return x + mu_ref[...] * (prev - x)

    r = shift(r_ref, rp_ref, mur_ref)
    k = shift(k_ref, kp_ref, muk_ref)
    v = shift(v_ref, vp_ref, muv_ref)
    wa = shift(wa_ref, wap_ref, muwa_ref)
    lane = lax.broadcasted_iota(jnp.int32, wa.shape, 1)
    lora_in = jnp.where(lane < LORA, jnp.tanh(wa), wa).astype(BF16)
    lora = _dot(lora_in, w2_ref[...])
    z = w0_ref[...] + lora[:, :WIDTH]
    wlog = -(jnp.maximum(-z, 0.0) + jnp.log1p(jnp.exp(-jnp.abs(z)))) - 0.5
    a = _sigmoid(a0_ref[...] + lora[:, WIDTH:])
    ro_ref[...] = r
    lwo_ref[...] = -jnp.exp(wlog)
    kto_ref[...] = k * (1.0 + (a - 1.0) * ka_ref[...])
    vo_ref[...] = v
    kko_ref[...] = k * kkk_ref[...]
    ao_ref[...] = a


def _dprep(p, mu, w0, w2, a0, kkk, ka, b, s):
    t = b * s
    rb = TM_D // 8
    c = COL_D // WIDTH
    row = lambda w, col: pl.BlockSpec((TM_D, w), lambda i: (i, col))
    prev = lambda w, col: pl.BlockSpec((8, w), lambda i: (jnp.maximum(i * rb - 1, 0), col))
    vec = lambda w: pl.BlockSpec((1, w), lambda i: (0, 0))
    mur, muk, muv = mu[:, 0:256], mu[:, 256:512], mu[:, 512:768]
    muwa = mu[:, 768:896]
    return pl.pallas_call(
        functools.partial(_dprep_kernel, seq=s),
        grid=(t // TM_D,),
        in_specs=[row(WIDTH, c), row(WIDTH, c + 1), row(WIDTH, c + 2), row(128, COL_DWA // 128),
                  prev(WIDTH, c), prev(WIDTH, c + 1), prev(WIDTH, c + 2), prev(128, COL_DWA // 128),
                  vec(WIDTH), vec(WIDTH), vec(WIDTH), vec(128), vec(WIDTH),
                  pl.BlockSpec((128, 2 * WIDTH), lambda i: (0, 0)), vec(WIDTH), vec(WIDTH), vec(WIDTH)],
        out_specs=[row(WIDTH, 0)] * 6,
        out_shape=[jax.ShapeDtypeStruct((t, WIDTH), F32)] * 6,
        compiler_params=_cparams(("parallel",)),
        name="rwkv_prep",
    )(p, p, p, p, p, p, p, p, mur, muk, muv, muwa, w0, w2, a0, kkk, ka)


def _rwkv_kernel(r_ref, lw_ref, kt_ref, v_ref, kk_ref, a_ref, rk_ref, gg_ref, gb_ref, o_ref, st_ref):
    @pl.when(pl.program_id(1) == 0)
    def _():
        st_ref[...] = jnp.zeros(st_ref.shape, F32)

    tt = lax.broadcasted_iota(jnp.int32, (TS_D, TS_D), 0)
    ss = lax.broadcasted_iota(jnp.int32, (TS_D, TS_D), 1)
    same = (tt // CHUNK) == (ss // CHUNK)
    strict = same & (tt > ss)
    incl = same & (tt >= ss)
    eye = jnp.where(tt == ss, 1.0, 0.0)
    ones_bd = jnp.where(same, 1.0, 0.0).astype(BF16)
    ltri_bd = jnp.where(incl, 1.0, 0.0).astype(BF16)
    hid = lax.broadcasted_iota(jnp.int32, (1, WIDTH), 1) // HEAD_DIM

    def group_sum(x):
        return _dot3(x, ones_bd, left=True)

    rr, lw, kt, vv, kkr, aa = (x[...] for x in (r_ref, lw_ref, kt_ref, v_ref, kk_ref, a_ref))
    kk = kkr / jnp.maximum(jnp.sqrt(group_sum(kkr * kkr)), 1e-12)
    be = kk * aa
    b = _dot3(ltri_bd, lw, left=False)
    bl = _dot3(ones_bd, lw, left=False)
    eb = jnp.exp(b)
    enb = jnp.exp(-b)
    dec = jnp.exp(bl - b)
    qa = -kk * jnp.exp(b - lw)
    qr = rr * eb
    q2 = jnp.concatenate([qa, qr], axis=0)
    k2 = jnp.concatenate([be * enb, kt * enb], axis=0).astype(BF16)
    zero = jnp.zeros((TS_D, WIDTH), F32)
    sa0, qt, qo, o0 = zero, zero, zero, zero
    for h in range(HEADS):
        hm = hid == h
        g = _dot_nt(jnp.where(hm, q2, 0.0).astype(BF16), k2)
        a_ab = jnp.where(strict, g[:TS_D, :TS_D], 0.0)
        a_ak = jnp.where(strict, g[:TS_D, TS_D:], 0.0).astype(BF16)
        a_rb = jnp.where(incl, g[TS_D:, :TS_D], 0.0).astype(BF16)
        a_rk = jnp.where(incl, g[TS_D:, TS_D:], 0.0).astype(BF16)
        tm = eye + a_ab
        pw = a_ab.astype(BF16)
        for _ in range(int(math.log2(CHUNK)) - 1):
            pwf = _dot(pw, pw)
            pw = pwf.astype(BF16)
            tm = tm + _dot(tm.astype(BF16), pw)
        tmb = tm.astype(BF16)
        vh = jnp.where(hm, vv, 0.0).astype(BF16)
        sa0_h = _dot(tmb, _dot(a_ak, vh).astype(BF16))
        qt_h = _dot(tmb, jnp.where(hm, qa, 0.0).astype(BF16))
        sa0 = sa0 + sa0_h
        qt = qt + qt_h
        qo = qo + jnp.where(hm, qr, 0.0) + _dot(a_rb, qt_h.astype(BF16))
        o0 = o0 + _dot(a_rb, sa0_h.astype(BF16)) + _dot(a_rk, vh)
    bd = be * dec
    kd = kt * dec
    outs = []
    for c in range(TS_D // CHUNK):
        rows = slice(c * CHUNK, (c + 1) * CHUNK)
        st = st_ref[...]
        stb = st.astype(BF16)
        sa_c = sa0[rows] + _dot_nt(qt[rows].astype(BF16), stb)
        outs.append(o0[rows] + _dot_nt(qo[rows].astype(BF16), stb))
        lhs_t = jnp.concatenate([sa_c, vv[rows]], axis=0).T.astype(BF16)
        rhs = jnp.concatenate([bd[rows], kd[rows]], axis=0).astype(BF16)
        upd = _dot(lhs_t, rhs)
        st_ref[...] = st * jnp.exp(bl[c * CHUNK:c * CHUNK + 1, :]) + jnp.where(same, upd, 0.0)
    o = jnp.concatenate(outs, axis=0)
    mean = group_sum(o) * (1.0 / HEAD_DIM)
    cen = o - mean
    var = group_sum(cen * cen) * (1.0 / HEAD_DIM)
    o = cen * lax.rsqrt(var + GN_EPS) * gg_ref[...] + gb_ref[...]
    o_ref[...] = o + group_sum(rr * kt * rk_ref[...]) * vv


def _rwkv7(r, lw, kt, v, kk, a, rk, gg, gb, b, s):
    nt = s // TS_D
    row = pl.BlockSpec((TS_D, WIDTH), lambda bb, i: (bb * nt + i, 0))
    vec = pl.BlockSpec((1, WIDTH), lambda bb, i: (0, 0))
    return pl.pallas_call(
        _rwkv_kernel,
        grid=(b, nt),
        in_specs=[row] * 6 + [vec] * 3,
        out_specs=row,
        out_shape=jax.ShapeDtypeStruct((b * s, WIDTH), F32),
        scratch_shapes=[pltpu.VMEM((WIDTH, WIDTH), F32)],
        compiler_params=_cparams(("parallel", "arbitrary")),
        name="rwkv7",
    )(r, lw, kt, v, kk, a, rk, gg, gb)


TM_M = 256


def _merge_kernel(oa_ref, ob_ref, oc_ref, od_ref, za_ref, zb_ref, zc_ref, zd_ref,
                  ga_ref, gb_ref, gc_ref, gd_ref, x_ref, wb_ref, wo_ref, fg_ref, o_ref, *, final):
    def branch(o_r, z_r, g_r, idx):
        z = z_r[...]
        u = (o_r[...] * (z * _sigmoid(z))).astype(BF16)
        return _sigmoid(g_r[...]) * _dot(u, wb_ref[idx])

    y = (branch(oa_ref, za_ref, ga_ref, 0) + branch(ob_ref, zb_ref, gb_ref, 1)
         + branch(oc_ref, zc_ref, gc_ref, 2) + branch(od_ref, zd_ref, gd_ref, 3))
    xn = x_ref[...] + _dot(y.astype(BF16), wo_ref[...])
    if final:
        inv = lax.rsqrt(jnp.mean(xn * xn, axis=-1, keepdims=True) + EPS)
        xn = (xn * inv) * fg_ref[...]
    o_ref[...] = xn


def _merge(oa, ob, oc, od, p, x2, wb, wo, fg, final):
    t = x2.shape[0]
    row = lambda w, col: pl.BlockSpec((TM_M, w), lambda i: (i, col))
    zcol = lambda base: (base + 3 * WIDTH) // WIDTH
    gcol = lambda n: COL_G // D_MODEL + n
    return pl.pallas_call(
        functools.partial(_merge_kernel, final=final),
        grid=(t // TM_M,),
        in_specs=[row(WIDTH, 0)] * 4
        + [row(WIDTH, zcol(COL_A)), row(WIDTH, zcol(COL_B)), row(WIDTH, zcol(COL_C)), row(WIDTH, COL_DZ // WIDTH)]
        + [row(D_MODEL, gcol(n)) for n in range(4)]
        + [row(D_MODEL, 0),
           pl.BlockSpec((4, WIDTH, D_MODEL), lambda i: (0, 0, 0)),
           pl.BlockSpec((D_MODEL, D_MODEL), lambda i: (0, 0)),
           pl.BlockSpec((1, D_MODEL), lambda i: (0, 0))],
        out_specs=row(D_MODEL, 0),
        out_shape=jax.ShapeDtypeStruct((t, D_MODEL), F32),
        compiler_params=_cparams(("parallel",)),
        name="merge",
    )(oa, ob, oc, od, p, p, p, p, p, p, p, p, x2, wb, wo, fg)


def _pad_w_in(w):
    def seg(lo, hi, width):
        s = w[:, lo:hi]
        return jnp.pad(s, ((0, 0), (0, width - (hi - lo))))
    c_end = 2048 + 1608
    d_end = c_end + 896
    parts = [w[:, :2048], seg(2048, c_end, COL_D - COL_C), seg(c_end, d_end, COL_DZ - COL_D), w[:, d_end:]]
    return jnp.concatenate(parts, axis=1).astype(BF16)


def kernel(x, norm_g, w_in, a_rel_bias, b_lb_logits, b_norm_g, d_mu, d_w0, d_w_up, d_a0, d_a_up,
           d_k_k, d_k_a, d_r_k, d_gn_g, d_gn_b, w_branch, w_out, final_g):
    b, s, d = x.shape
    depth = norm_g.shape[0]
    t = b * s
    x2 = x.reshape(t, d)
    cos, sin = _rope_tables(s)
    lb_all = jnp.cumsum(jax.nn.softmax(b_lb_logits.astype(F32), axis=0), axis=0)
    lb_all = lb_all - lb_all[0:1]
    tm_in = 512 if t % 512 == 0 else 256
    for l in range(depth):
        p = _inproj(x2, norm_g[l][None, :], _pad_w_in(w_in[l]), tm_in, 1024)
        oa = _band_attention(p, _band_bias(a_rel_bias[l]), b, s)
        lb = lb_all[l][None, :]
        ob = _hgrn2(p, jnp.log(lb), jnp.log1p(-lb), 1.0 - lb, b_norm_g[l][None, :], b, s)
        qr, kr, vb, qir, kir = _rope_prep(p, cos, sin, b, s)
        oc = _dsa(p, qr, kr, vb, qir, kir, b, s)
        zeros = jnp.zeros((LORA, WIDTH), F32)
        w2 = jnp.concatenate([jnp.concatenate([d_w_up[l], zeros], axis=1),
                              jnp.concatenate([zeros, d_a_up[l]], axis=1)], axis=0).astype(BF16)
        r, lw, kt, v, kk, a = _dprep(p, d_mu[l][None, :], d_w0[l][None, :], w2, d_a0[l][None, :],
                                     d_k_k[l][None, :], d_k_a[l][None, :], b, s)
        od = _rwkv7(r, lw, kt, v, kk, a, d_r_k[l].reshape(1, WIDTH), d_gn_g[l][None, :], d_gn_b[l][None, :], b, s)
        x2 = _merge(oa, ob, oc, od, p, x2, w_branch[l].astype(BF16), w_out[l].astype(BF16),
                    final_g[None, :], l == depth - 1)
    return x2.reshape(b, s, d)
```

```python
import functools
import math

import numpy as np
import jax
import jax.numpy as jnp
from jax import lax
from jax.experimental import pallas as pl
from jax.experimental.pallas import tpu as pltpu

F32 = jnp.float32
BF16 = jnp.bfloat16
HI = lax.Precision.HIGHEST

D_MODEL = 1024
HEADS = 4
HEAD_DIM = 64
WIDTH = HEADS * HEAD_DIM
CHUNK = 64
A_LEFT_CHUNKS = 8
REL_CLIP = 128
IDX_HEADS = 8
IDX_DIM = 64
TOPK_MAX = 256
LORA = 64
ROPE_THETA = 10000.0
EPS = 1e-6
GN_EPS = 64e-5
NEG = -1e30
INT_MIN = -2 ** 31
I16 = jnp.int16
I16_MIN = -2 ** 15

COL_A = 0
COL_B = 1024
COL_C = 2048
COL_CQI = 3072
COL_CKW = 3584
COL_D = 3840
COL_DWA = 4608
COL_DZ = 4864
COL_G = 5120
NP = 9216
VMEM_LIMIT = 56 * 1024 * 1024

_NT = (((1,), (1,)), ((), ()))


def _sigmoid(x):
    return 1.0 / (1.0 + jnp.exp(-x))


def _dot(a, b, precision=None):
    return jnp.dot(a, b, preferred_element_type=F32, precision=precision)


def _dot_nt(a, b, precision=None):
    return lax.dot_general(a, b, _NT, preferred_element_type=F32, precision=precision)


def _dot3(a, b, left):
    x = a if left else b
    hi = x.astype(BF16)
    rest = x - hi.astype(F32)
    mid = rest.astype(BF16)
    lo = (rest - mid.astype(F32)).astype(BF16)
    if left:
        return _dot(hi, b) + _dot(mid, b) + _dot(lo, b)
    return _dot(a, hi) + _dot(a, mid) + _dot(a, lo)


def _cparams(sem):
    return pltpu.CompilerParams(dimension_semantics=sem, vmem_limit_bytes=VMEM_LIMIT)


def _inproj_kernel(x_ref, g_ref, w_ref, o_ref, h_ref):
    @pl.when(pl.program_id(1) == 0)
    def _():
        x = x_ref[...]
        inv = lax.rsqrt(jnp.mean(x * x, axis=-1, keepdims=True) + EPS)
        h_ref[...] = ((x * inv) * g_ref[...]).astype(BF16)

    o_ref[...] = _dot(h_ref[...], w_ref[...])


def _inproj(x2, g, w_pad, tm, tn):
    t = x2.shape[0]
    return pl.pallas_call(
        _inproj_kernel,
        grid=(t // tm, NP // tn),
        in_specs=[
            pl.BlockSpec((tm, D_MODEL), lambda i, j: (i, 0)),
            pl.BlockSpec((1, D_MODEL), lambda i, j: (0, 0)),
            pl.BlockSpec((D_MODEL, tn), lambda i, j: (0, j)),
        ],
        out_specs=pl.BlockSpec((tm, tn), lambda i, j: (i, j)),
        out_shape=jax.ShapeDtypeStruct((t, NP), F32),
        scratch_shapes=[pltpu.VMEM((tm, D_MODEL), BF16)],
        compiler_params=_cparams(("parallel", "arbitrary")),
        name="inproj",
    )(x2, g, w_pad)


TQ_A = 256
WIN_A = 3 * TQ_A


def _band_kernel(q_ref, k0_ref, k1_ref, k2_ref, v0_ref, v1_ref, v2_ref, bias_ref, o_ref):
    i = pl.program_id(1)
    q = q_ref[...]
    kc = jnp.concatenate([k0_ref[...], k1_ref[...], k2_ref[...]], axis=0).astype(BF16)
    vc = jnp.concatenate([v0_ref[...], v1_ref[...], v2_ref[...]], axis=0).astype(BF16)
    lane = lax.broadcasted_iota(jnp.int32, (1, WIDTH), 1)
    kpos = lax.broadcasted_iota(jnp.int32, (1, WIN_A), 1) + (i - 2) * TQ_A
    kvalid = kpos >= 0
    out = jnp.zeros((TQ_A, WIDTH), F32)
    for h in range(HEADS):
        hm = (lane // HEAD_DIM) == h
        qh = jnp.where(hm, q, 0.0).astype(BF16)
        s = _dot_nt(qh, kc) * (HEAD_DIM ** -0.5) + bias_ref[h]
        s = jnp.where(kvalid, s, NEG)
        m = jnp.max(s, axis=-1, keepdims=True)
        p = jnp.exp(s - m)
        l = jnp.sum(p, axis=-1, keepdims=True)
        o = _dot(p.astype(BF16), vc) / l
        out = jnp.where(hm, o, out)
    o_ref[...] = out


def _band_attention(p, bias, b, s):
    nt = s // TQ_A
    qcol, kcol, vcol = COL_A // WIDTH, COL_A // WIDTH + 1, COL_A // WIDTH + 2

    def kv_spec(col, back):
        return pl.BlockSpec((TQ_A, WIDTH), lambda bb, i: (bb * nt + jnp.maximum(i - back, 0), col))

    return pl.pallas_call(
        _band_kernel,
        grid=(b, nt),
        in_specs=[
            pl.BlockSpec((TQ_A, WIDTH), lambda bb, i: (bb * nt + i, qcol)),
            kv_spec(kcol, 2), kv_spec(kcol, 1), kv_spec(kcol, 0),
            kv_spec(vcol, 2), kv_spec(vcol, 1), kv_spec(vcol, 0),
            pl.BlockSpec((HEADS, TQ_A, WIN_A), lambda bb, i: (0, 0, 0)),
        ],
        out_specs=pl.BlockSpec((TQ_A, WIDTH), lambda bb, i: (bb * nt + i, 0)),
        out_shape=jax.ShapeDtypeStruct((b * s, WIDTH), F32),
        compiler_params=_cparams(("parallel", "parallel")),
        name="band_attention",
    )(p, p, p, p, p, p, p, bias)


def _band_bias(rel_bias):
    tab = rel_bias.astype(F32)
    n = tab.shape[1]
    d_lo, d_hi = 2 * TQ_A - (WIN_A - 1), 2 * TQ_A + TQ_A - 1
    n_lo = -(CHUNK - 1) - d_lo + 1
    n_hi = d_hi - REL_CLIP + 1
    v = jnp.concatenate([jnp.repeat(tab[:, :1], n_lo, axis=1), tab[:, 1:n - 1],
                         jnp.repeat(tab[:, n - 1:], n_hi, axis=1)], axis=1)
    span = WIN_A + TQ_A
    u = jnp.concatenate([v[:, :WIN_A][:, ::-1], jnp.zeros((HEADS, span - v.shape[1]), F32),
                         v[:, WIN_A:][:, ::-1]], axis=1)
    skew = jnp.tile(u, (1, TQ_A))[:, :TQ_A * (span - 1)].reshape(HEADS, TQ_A, span - 1)[:, :, :WIN_A]
    ql = np.arange(TQ_A)[:, None]
    kl = np.arange(WIN_A)[None, :]
    lo = CHUNK * (ql // CHUNK)
    band = (kl >= lo) & (kl < lo + (A_LEFT_CHUNKS + 1) * CHUNK)
    return jnp.where(band[None], skew, NEG)


TS_B = 256
_LEVELS = (1, 2, 4, 8, 16, 32)


def _hgrn_matrix():
    t = np.arange(CHUNK)
    ltri = (t[:, None] >= t[None, :]).astype(np.float32)
    mats = [ltri]
    for m in _LEVELS:
        ref = (t // (2 * m)) * (2 * m) + m - 1
        mats.append(ltri[ref])
    return np.concatenate(mats, axis=0)


def _hgrn_kernel(q_ref, f_ref, i_ref, la_ref, l1_ref, oml_ref, ng_ref, m_ref, o_ref, st_ref):
    @pl.when(pl.program_id(1) == 0)
    def _():
        st_ref[...] = jnp.zeros(st_ref.shape, F32)

    fl = f_ref[...]
    logsig = jnp.minimum(fl, 0.0) - jnp.log1p(jnp.exp(-jnp.abs(fl)))
    y = l1_ref[...] + logsig
    la = la_ref[...]
    g = jnp.maximum(la, y) + jnp.log1p(jnp.exp(-jnp.abs(la - y)))
    kk = oml_ref[...] * _sigmoid(-fl)
    qq = q_ref[...]
    qf = qq * _sigmoid(qq)
    vv = i_ref[...]
    mat = m_ref[...]
    ng = ng_ref[...]
    tt = lax.broadcasted_iota(jnp.int32, (CHUNK, CHUNK), 0)
    ss = lax.broadcasted_iota(jnp.int32, (CHUNK, CHUNK), 1)
    for c in range(TS_B // CHUNK):
        r0 = c * CHUNK
        ball = _dot(mat, g[r0:r0 + CHUNK], HI)
        for h in range(HEADS):
            c0 = h * HEAD_DIM
            bh = ball[0:CHUNK, c0:c0 + HEAD_DIM]
            qh = qf[r0:r0 + CHUNK, c0:c0 + HEAD_DIM]
            kh = kk[r0:r0 + CHUNK, c0:c0 + HEAD_DIM]
            vh = vv[r0:r0 + CHUNK, c0:c0 + HEAD_DIM]
            att = jnp.where(tt == ss, _dot_nt(qh.astype(BF16), kh.astype(BF16)), 0.0)
            for li, m in enumerate(_LEVELS):
                br = ball[(li + 1) * CHUNK:(li + 2) * CHUNK, c0:c0 + HEAD_DIM]
                ql = qh * jnp.exp(jnp.minimum(bh - br, 0.0))
                kl = kh * jnp.exp(jnp.minimum(br - bh, 0.0))
                mask = ((tt // (2 * m)) == (ss // (2 * m))) & ((tt % (2 * m)) >= m) & ((ss % (2 * m)) < m)
                att = att + jnp.where(mask, _dot_nt(ql.astype(BF16), kl.astype(BF16)), 0.0)
            st = st_ref[h]
            o = _dot(att.astype(BF16), vh.astype(BF16))
            o = o + _dot_nt((qh * jnp.exp(bh)).astype(BF16), st.astype(BF16))
            bl = bh[CHUNK - 1:CHUNK, :]
            kd = kh * jnp.exp(bl - bh)
            st_ref[h] = st * jnp.exp(bl) + _dot(vh.T.astype(BF16), kd.astype(BF16))
            o = o * lax.rsqrt(jnp.mean(o * o, axis=-1, keepdims=True) + EPS) * ng
            o_ref[r0:r0 + CHUNK, c0:c0 + HEAD_DIM] = o


def _hgrn2(p, la, l1, oml, ng, b, s):
    nt = s // TS_B
    c = COL_B // WIDTH
    row = lambda col: pl.BlockSpec((TS_B, WIDTH), lambda bb, i: (bb * nt + i, col))
    vec = lambda w: pl.BlockSpec((1, w), lambda bb, i: (0, 0))
    mat = jnp.asarray(_hgrn_matrix())
    return pl.pallas_call(
        _hgrn_kernel,
        grid=(b, nt),
        in_specs=[row(c), row(c + 1), row(c + 2), vec(WIDTH), vec(WIDTH), vec(WIDTH), vec(HEAD_DIM),
                  pl.BlockSpec(mat.shape, lambda bb, i: (0, 0))],
        out_specs=pl.BlockSpec((TS_B, WIDTH), lambda bb, i: (bb * nt + i, 0)),
        out_shape=jax.ShapeDtypeStruct((b * s, WIDTH), F32),
        scratch_shapes=[pltpu.VMEM((HEADS, HEAD_DIM, HEAD_DIM), F32)],
        compiler_params=_cparams(("parallel", "arbitrary")),
        name="hgrn2",
    )(p, p, p, la, l1, oml, ng, mat)


TM_R = 256
TQ_C = 128
KB_C = 512
DSA_GROUPS = 1


def _rope(x, cos, sin_signed):
    w = x.shape[-1]
    lane = lax.broadcasted_iota(jnp.int32, x.shape, 1)
    low = (lane % HEAD_DIM) < (HEAD_DIM // 2)
    swapped = jnp.where(low, pltpu.roll(x, w - HEAD_DIM // 2, 1), pltpu.roll(x, HEAD_DIM // 2, 1))
    return x * cos + swapped * sin_signed


def _rope_kernel(q_ref, k_ref, v_ref, qi_ref, kw_ref, cos_ref, sin_ref, qo_ref, ko_ref, vo_ref, qio_ref, kio_ref):
    cos = cos_ref[...]
    sin = sin_ref[...]
    qo_ref[...] = _rope(q_ref[...], cos, sin).astype(BF16)
    ko_ref[...] = _rope(k_ref[...], cos, sin).astype(BF16)
    vo_ref[...] = v_ref[...].astype(BF16)
    cos2 = jnp.concatenate([cos, cos], axis=1)
    sin2 = jnp.concatenate([sin, sin], axis=1)
    qio_ref[...] = _rope(qi_ref[...], cos2, sin2).astype(BF16)
    kr = _rope(kw_ref[...], cos[:, :128], sin[:, :128])
    lane = lax.broadcasted_iota(jnp.int32, kr.shape, 1)
    kio_ref[...] = jnp.where(lane < IDX_DIM, kr, pltpu.roll(kr, IDX_DIM, 1)).astype(BF16)


def _rope_prep(p, cos, sin, b, s):
    t = b * s
    nt = s // TM_R
    c = COL_C // WIDTH
    row = lambda w, col: pl.BlockSpec((TM_R, w), lambda i: (i, col))
    tab = pl.BlockSpec((TM_R, WIDTH), lambda i: (i % nt, 0))
    return pl.pallas_call(
        _rope_kernel,
        grid=(t // TM_R,),
        in_specs=[row(WIDTH, c), row(WIDTH, c + 1), row(WIDTH, c + 2),
                  row(512, COL_CQI // 512), row(128, COL_CKW // 128), tab, tab],
        out_specs=[row(WIDTH, 0), row(WIDTH, 0), row(WIDTH, 0), row(512, 0), row(128, 0)],
        out_shape=[jax.ShapeDtypeStruct((t, WIDTH), BF16)] * 3
        + [jax.ShapeDtypeStruct((t, 512), BF16), jax.ShapeDtypeStruct((t, 128), BF16)],
        compiler_params=_cparams(("parallel",)),
        name="rope_prep",
    )(p, p, p, p, p, cos, sin)


def _rope_tables(s):
    half = HEAD_DIM // 2
    freqs = ROPE_THETA ** (-jnp.arange(half, dtype=F32) / half)
    ang = jnp.arange(s, dtype=jnp.int32).astype(F32)[:, None] * freqs[None, :]
    cos = jnp.cos(ang)
    sin = jnp.sin(ang)
    cos = jnp.tile(jnp.concatenate([cos, cos], axis=1), (1, HEADS))
    sin = jnp.tile(jnp.concatenate([-sin, sin], axis=1), (1, HEADS))
    return cos, sin


def _dsa_kernel(q_ref, k_ref, v_ref, qi_ref, ki_ref, w_ref, o_ref, hi_ref, lo_ref, *, topk, idx_bits):
    i = pl.program_id(1)
    nkb = (i * TQ_C) // KB_C + 1
    lane128 = lax.broadcasted_iota(jnp.int32, (1, 128), 1)
    lane256 = lax.broadcasted_iota(jnp.int32, (1, WIDTH), 1)
    kiota = lax.broadcasted_iota(jnp.int32, (1, KB_C), 1)
    qchunk = (i * TQ_C + lax.broadcasted_iota(jnp.int32, (TQ_C, 1), 0)) // CHUNK
    w = w_ref[...] * (IDX_HEADS ** -0.5)
    qi = qi_ref[...] * jnp.asarray(IDX_DIM ** -0.5, BF16)
    qi_heads = [jnp.where((lane128 // IDX_DIM) == (h % 2), qi[:, 128 * (h // 2):128 * (h // 2 + 1)],
                          jnp.zeros((), BF16)) for h in range(IDX_HEADS)]

    def score_block(kb, carry):
        off = pl.multiple_of(kb * KB_C, KB_C)
        kib = ki_ref[pl.ds(off, KB_C), :]
        acc = jnp.zeros((TQ_C, KB_C), F32)
        for h in range(IDX_HEADS):
            acc = acc + jnp.maximum(_dot_nt(qi_heads[h], kib), 0.0) * w[:, IDX_DIM + h:IDX_DIM + h + 1]
        bits = lax.bitcast_convert_type(acc, jnp.int32)
        key = bits ^ ((bits >> 31) & jnp.int32(0x7FFFFFFF))
        key = jnp.where(acc == 0.0, 0, key)
        adm = ((off + kiota) // CHUNK) <= qchunk
        key = jnp.where(adm, key, INT_MIN)
        hi_ref[kb] = (key >> 16).astype(I16)
        lo_ref[kb] = ((key & 0xFFFF) - 32768).astype(I16)
        return carry

    lax.fori_loop(0, nkb, score_block, 0)

    one16 = jnp.ones((), I16)
    zero16 = jnp.zeros((), I16)

    def count(pred):
        def body(kb, acc):
            c = jnp.where(pred(kb), one16, zero16)
            return acc + (c[:, 0:128] + c[:, 128:256] + c[:, 256:384] + c[:, 384:512])
        acc = lax.fori_loop(0, nkb, body, jnp.zeros((TQ_C, 128), I16))
        return jnp.sum(acc.astype(F32), axis=1, keepdims=True)

    def descend(ref, kth):
        def bit(bi, t):
            cand = t + lax.shift_left(jnp.int32(1), 15 - bi)
            c16 = cand.astype(I16)
            return jnp.where(count(lambda kb: ref[kb] >= c16) >= kth, cand, t)
        return lax.fori_loop(0, 16, bit, jnp.full((TQ_C, 1), I16_MIN, jnp.int32))

    tau_hi32 = descend(hi_ref, topk)
    tau_hi = tau_hi32.astype(I16)
    kth_lo = topk - count(lambda kb: hi_ref[kb] > tau_hi)

    def mask_low(kb, carry):
        lo_ref[kb] = jnp.where(hi_ref[kb] == tau_hi, lo_ref[kb], jnp.asarray(I16_MIN, I16))
        return carry

    lax.fori_loop(0, nkb, mask_low, 0)
    tau_lo = descend(lo_ref, kth_lo).astype(I16)
    need = kth_lo - count(lambda kb: lo_ref[kb] > tau_lo)
    real = tau_hi32 > I16_MIN
    real16 = tau_hi > jnp.asarray(I16_MIN, I16)

    def is_tie(kb):
        return (hi_ref[kb] == tau_hi) & (lo_ref[kb] == tau_lo)

    ties = count(is_tie)

    def index_search():
        def index_bit(bi, jmax):
            cand = jmax + lax.shift_left(jnp.int32(1), idx_bits - 1 - bi)
            c16 = cand.astype(I16)
            cnt = count(lambda kb: is_tie(kb) & ((kb * KB_C + kiota).astype(I16) < c16))
            return jnp.where(cnt < need, cand, jmax)
        return lax.fori_loop(0, idx_bits, index_bit, jnp.zeros((TQ_C, 1), jnp.int32))

    excess = jnp.max(jnp.where((ties > need) & real, 1.0, 0.0)) > 0.5
    jmax = lax.cond(excess, index_search, lambda: jnp.full((TQ_C, 1), 2 ** (idx_bits - 1), jnp.int32))
    jmax = jmax.astype(I16)

    q = q_ref[...] * jnp.asarray(HEAD_DIM ** -0.5, BF16)
    hpg = HEADS // DSA_GROUPS
    q_groups = [jnp.concatenate([jnp.where((lane256 // HEAD_DIM) == h, q, jnp.zeros((), BF16))
                                 for h in range(g * hpg, (g + 1) * hpg)], axis=0) for g in range(DSA_GROUPS)]
    def attend_block(kb, carry):
        off = pl.multiple_of(kb * KB_C, KB_C)
        kblk = k_ref[pl.ds(off, KB_C), :]
        vblk = v_ref[pl.ds(off, KB_C), :]
        hi = hi_ref[kb]
        lo = lo_ref[kb]
        in_order = (off + kiota).astype(I16) <= jmax
        sel = (hi > tau_hi) | ((hi == tau_hi) & real16 & ((lo > tau_lo) | ((lo == tau_lo) & in_order)))
        madd = (jnp.where(sel, one16, zero16).astype(F32) - 1.0) * (-NEG)
        new = []
        for g in range(DSA_GROUPS):
            m_old, l_old, acc_old = carry[g]
            s = jnp.concatenate([_dot_nt(q_groups[g], kblk[:KB_C // 2]), _dot_nt(q_groups[g], kblk[KB_C // 2:])],
                                axis=1) + jnp.concatenate([madd] * hpg, axis=0)
            m_new = jnp.maximum(m_old, jnp.max(s, axis=-1, keepdims=True))
            p = jnp.exp(s - m_new)
            corr = jnp.exp(m_old - m_new)
            l_new = corr * l_old + jnp.sum(p, axis=-1, keepdims=True)
            pb = p.astype(BF16)
            pv = jnp.concatenate([_dot(pb, vblk[:, :WIDTH // 2]), _dot(pb, vblk[:, WIDTH // 2:])], axis=1)
            new.append((m_new, l_new, corr * acc_old + pv))
        return tuple(new)

    rows = hpg * TQ_C
    init = tuple((jnp.full((rows, 1), NEG, F32), jnp.zeros((rows, 1), F32), jnp.zeros((rows, WIDTH), F32))
                 for _ in range(DSA_GROUPS))
    state = lax.fori_loop(0, nkb, attend_block, init)
    out = jnp.zeros((TQ_C, WIDTH), F32)
    for h in range(HEADS):
        _, l_fin, acc_fin = state[h // hpg]
        r0 = (h % hpg) * TQ_C
        out = jnp.where((lane256 // HEAD_DIM) == h, acc_fin[r0:r0 + TQ_C] / l_fin[r0:r0 + TQ_C], out)
    o_ref[...] = out


def _dsa(p, qr, kr, vb, qir, kir, b, s):
    nt = s // TQ_C
    topk = min(TOPK_MAX, s // 4)
    idx_bits = int(math.log2(s)) + 1
    qrow = lambda w, col: pl.BlockSpec((TQ_C, w), lambda bb, i: (bb * nt + i, col))
    full = lambda w: pl.BlockSpec((s, w), lambda bb, i: (bb, 0))
    return pl.pallas_call(
        functools.partial(_dsa_kernel, topk=topk, idx_bits=idx_bits),
        grid=(b, nt),
        in_specs=[qrow(WIDTH, 0), full(WIDTH), full(WIDTH), qrow(512, 0), full(128),
                  qrow(128, COL_CKW // 128)],
        out_specs=qrow(WIDTH, 0),
        out_shape=jax.ShapeDtypeStruct((b * s, WIDTH), F32),
        scratch_shapes=[
            pltpu.VMEM((s // KB_C, TQ_C, KB_C), I16),
            pltpu.VMEM((s // KB_C, TQ_C, KB_C), I16),
        ],
        compiler_params=_cparams(("parallel", "arbitrary")),
        name="dsa",
    )(qr, kr, vb, qir, kir, p)


TM_D = 256
TS_D = 256
assert TS_D == WIDTH and CHUNK == HEAD_DIM


def _dprep_kernel(r_ref, k_ref, v_ref, wa_ref, rp_ref, kp_ref, vp_ref, wap_ref,
                  mur_ref, muk_ref, muv_ref, muwa_ref, w0_ref, w2_ref, a0_ref, kkk_ref, ka_ref,
                  ro_ref, lwo_ref, kto_ref, vo_ref, kko_ref, ao_ref, *, seq):
    first = (pl.program_id(0) * TM_D) % seq == 0

    def shift(x_ref, p_ref, mu_ref):
        x = x_ref[...]
        row0 = jnp.where(first, 0.0, p_ref[7:8, :])
        rid = lax.broadcasted_iota(jnp.int32, x.shape, 0)
        prev = jnp.where(rid == 0, row0, pltpu.roll(x, 1, 0))
        return x + mu_ref[...] * (prev - x)

    r = shift(r_ref, rp_ref, mur_ref)
    k = shift(k_ref, kp_ref, muk_ref)
    v = shift(v_ref, vp_ref, muv_ref)
    wa = shift(wa_ref, wap_ref, muwa_ref)
    lane = lax.broadcasted_iota(jnp.int32, wa.shape, 1)
    lora_in = jnp.where(lane < LORA, jnp.tanh(wa), wa).astype(BF16)
    lora = _dot(lora_in, w2_ref[...])
    z = w0_ref[...] + lora[:, :WIDTH]
    wlog = -(jnp.maximum(-z, 0.0) + jnp.log1p(jnp.exp(-jnp.abs(z)))) - 0.5
    a = _sigmoid(a0_ref[...] + lora[:, WIDTH:])
    ro_ref[...] = r
    lwo_ref[...] = -jnp.exp(wlog)
    kto_ref[...] = k * (1.0 + (a - 1.0) * ka_ref[...])
    vo_ref[...] = v
    kko_ref[...] = k * kkk_ref[...]
    ao_ref[...] = a


def _dprep(p, mu, w0, w2, a0, kkk, ka, b, s):
    t = b * s
    rb = TM_D // 8
    c = COL_D // WIDTH
    row = lambda w, col: pl.BlockSpec((TM_D, w), lambda i: (i, col))
    prev = lambda w, col: pl.BlockSpec((8, w), lambda i: (jnp.maximum(i * rb - 1, 0), col))
    vec = lambda w: pl.BlockSpec((1, w), lambda i: (0, 0))
    mur, muk, muv = mu[:, 0:256], mu[:, 256:512], mu[:, 512:768]
    muwa = mu[:, 768:896]
    return pl.pallas_call(
        functools.partial(_dprep_kernel, seq=s),
        grid=(t // TM_D,),
        in_specs=[row(WIDTH, c), row(WIDTH, c + 1), row(WIDTH, c + 2), row(128, COL_DWA // 128),
                  prev(WIDTH, c), prev(WIDTH, c + 1), prev(WIDTH, c + 2), prev(128, COL_DWA // 128),
                  vec(WIDTH), vec(WIDTH), vec(WIDTH), vec(128), vec(WIDTH),
                  pl.BlockSpec((128, 2 * WIDTH), lambda i: (0, 0)), vec(WIDTH), vec(WIDTH), vec(WIDTH)],
        out_specs=[row(WIDTH, 0)] * 6,
        out_shape=[jax.ShapeDtypeStruct((t, WIDTH), F32)] * 6,
        compiler_params=_cparams(("parallel",)),
        name="rwkv_prep",
    )(p, p, p, p, p, p, p, p, mur, muk, muv, muwa, w0, w2, a0, kkk, ka)


def _rwkv_kernel(r_ref, lw_ref, kt_ref, v_ref, kk_ref, a_ref, rk_ref, gg_ref, gb_ref, o_ref, st_ref):
    @pl.when(pl.program_id(1) == 0)
    def _():
        st_ref[...] = jnp.zeros(st_ref.shape, F32)

    tt = lax.broadcasted_iota(jnp.int32, (TS_D, TS_D), 0)
    ss = lax.broadcasted_iota(jnp.int32, (TS_D, TS_D), 1)
    same = (tt // CHUNK) == (ss // CHUNK)
    strict = same & (tt > ss)
    incl = same & (tt >= ss)
    eye = jnp.where(tt == ss, 1.0, 0.0)
    ones_bd = jnp.where(same, 1.0, 0.0).astype(BF16)
    ltri_bd = jnp.where(incl, 1.0, 0.0).astype(BF16)
    hid = lax.broadcasted_iota(jnp.int32, (1, WIDTH), 1) // HEAD_DIM

    def group_sum(x):
        return _dot3(x, ones_bd, left=True)

    rr, lw, kt, vv, kkr, aa = (x[...] for x in (r_ref, lw_ref, kt_ref, v_ref, kk_ref, a_ref))
    kk = kkr / jnp.maximum(jnp.sqrt(group_sum(kkr * kkr)), 1e-12)
    be = kk * aa
    b = _dot3(ltri_bd, lw, left=False)
    bl = _dot3(ones_bd, lw, left=False)
    eb = jnp.exp(b)
    enb = jnp.exp(-b)
    dec = jnp.exp(bl - b)
    qa = -kk * jnp.exp(b - lw)
    qr = rr * eb
    q2 = jnp.concatenate([qa, qr], axis=0)
    k2 = jnp.concatenate([be * enb, kt * enb], axis=0).astype(BF16)
    zero = jnp.zeros((TS_D, WIDTH), F32)
    sa0, qt, qo, o0 = zero, zero, zero, zero
    for h in range(HEADS):
        hm = hid == h
        g = _dot_nt(jnp.where(hm, q2, 0.0).astype(BF16), k2)
        a_ab = jnp.where(strict, g[:TS_D, :TS_D], 0.0)
        a_ak = jnp.where(strict, g[:TS_D, TS_D:], 0.0).astype(BF16)
        a_rb = jnp.where(incl, g[TS_D:, :TS_D], 0.0).astype(BF16)
        a_rk = jnp.where(incl, g[TS_D:, TS_D:], 0.0).astype(BF16)
        tm = eye + a_ab
        pw = a_ab.astype(BF16)
        for _ in range(int(math.log2(CHUNK)) - 1):
            pwf = _dot(pw, pw)
            pw = pwf.astype(BF16)
            tm = tm + _dot(tm.astype(BF16), pw)
        tmb = tm.astype(BF16)
        vh = jnp.where(hm, vv, 0.0).astype(BF16)
        sa0_h = _dot(tmb, _dot(a_ak, vh).astype(BF16))
        qt_h = _dot(tmb, jnp.where(hm, qa, 0.0).astype(BF16))
        sa0 = sa0 + sa0_h
        qt = qt + qt_h
        qo = qo + jnp.where(hm, qr, 0.0) + _dot(a_rb, qt_h.astype(BF16))
        o0 = o0 + _dot(a_rb, sa0_h.astype(BF16)) + _dot(a_rk, vh)
    bd = be * dec
    kd = kt * dec
    outs = []
    for c in range(TS_D // CHUNK):
        rows = slice(c * CHUNK, (c + 1) * CHUNK)
        st = st_ref[...]
        stb = st.astype(BF16)
        sa_c = sa0[rows] + _dot_nt(qt[rows].astype(BF16), stb)
        outs.append(o0[rows] + _dot_nt(qo[rows].astype(BF16), stb))
        lhs_t = jnp.concatenate([sa_c, vv[rows]], axis=0).T.astype(BF16)
        rhs = jnp.concatenate([bd[rows], kd[rows]], axis=0).astype(BF16)
        upd = _dot(lhs_t, rhs)
        st_ref[...] = st * jnp.exp(bl[c * CHUNK:c * CHUNK + 1, :]) + jnp.where(same, upd, 0.0)
    o = jnp.concatenate(outs, axis=0)
    mean = group_sum(o) * (1.0 / HEAD_DIM)
    cen = o - mean
    var = group_sum(cen * cen) * (1.0 / HEAD_DIM)
    o = cen * lax.rsqrt(var + GN_EPS) * gg_ref[...] + gb_ref[...]
    o_ref[...] = o + group_sum(rr * kt * rk_ref[...]) * vv


def _rwkv7(r, lw, kt, v, kk, a, rk, gg, gb, b, s):
    nt = s // TS_D
    row = pl.BlockSpec((TS_D, WIDTH), lambda bb, i: (bb * nt + i, 0))
    vec = pl.BlockSpec((1, WIDTH), lambda bb, i: (0, 0))
    return pl.pallas_call(
        _rwkv_kernel,
        grid=(b, nt),
        in_specs=[row] * 6 + [vec] * 3,
        out_specs=row,
        out_shape=jax.ShapeDtypeStruct((b * s, WIDTH), F32),
        scratch_shapes=[pltpu.VMEM((WIDTH, WIDTH), F32)],
        compiler_params=_cparams(("parallel", "arbitrary")),
        name="rwkv7",
    )(r, lw, kt, v, kk, a, rk, gg, gb)


TM_M = 256


def _merge_kernel(oa_ref, ob_ref, oc_ref, od_ref, za_ref, zb_ref, zc_ref, zd_ref,
                  ga_ref, gb_ref, gc_ref, gd_ref, x_ref, wb_ref, wo_ref, fg_ref, o_ref, *, final):
    def branch(o_r, z_r, g_r, idx):
        z = z_r[...]
        u = (o_r[...] * (z * _sigmoid(z))).astype(BF16)
        return _sigmoid(g_r[...]) * _dot(u, wb_ref[idx])

    y = (branch(oa_ref, za_ref, ga_ref, 0) + branch(ob_ref, zb_ref, gb_ref, 1)
         + branch(oc_ref, zc_ref, gc_ref, 2) + branch(od_ref, zd_ref, gd_ref, 3))
    xn = x_ref[...] + _dot(y.astype(BF16), wo_ref[...])
    if final:
        inv = lax.rsqrt(jnp.mean(xn * xn, axis=-1, keepdims=True) + EPS)
        xn = (xn * inv) * fg_ref[...]
    o_ref[...] = xn


def _merge(oa, ob, oc, od, p, x2, wb, wo, fg, final):
    t = x2.shape[0]
    row = lambda w, col: pl.BlockSpec((TM_M, w), lambda i: (i, col))
    zcol = lambda base: (base + 3 * WIDTH) // WIDTH
    gcol = lambda n: COL_G // D_MODEL + n
    return pl.pallas_call(
        functools.partial(_merge_kernel, final=final),
        grid=(t // TM_M,),
        in_specs=[row(WIDTH, 0)] * 4
        + [row(WIDTH, zcol(COL_A)), row(WIDTH, zcol(COL_B)), row(WIDTH, zcol(COL_C)), row(WIDTH, COL_DZ // WIDTH)]
        + [row(D_MODEL, gcol(n)) for n in range(4)]
        + [row(D_MODEL, 0),
           pl.BlockSpec((4, WIDTH, D_MODEL), lambda i: (0, 0, 0)),
           pl.BlockSpec((D_MODEL, D_MODEL), lambda i: (0, 0)),
           pl.BlockSpec((1, D_MODEL), lambda i: (0, 0))],
        out_specs=row(D_MODEL, 0),
        out_shape=jax.ShapeDtypeStruct((t, D_MODEL), F32),
        compiler_params=_cparams(("parallel",)),
        name="merge",
    )(oa, ob, oc, od, p, p, p, p, p, p, p, p, x2, wb, wo, fg)


def _pad_w_in(w):
    def seg(lo, hi, width):
        s = w[:, lo:hi]
        return jnp.pad(s, ((0, 0), (0, width - (hi - lo))))
    c_end = 2048 + 1608
    d_end = c_end + 896
    parts = [w[:, :2048], seg(2048, c_end, COL_D - COL_C), seg(c_end, d_end, COL_DZ - COL_D), w[:, d_end:]]
    return jnp.concatenate(parts, axis=1).astype(BF16)


def kernel(x, norm_g, w_in, a_rel_bias, b_lb_logits, b_norm_g, d_mu, d_w0, d_w_up, d_a0, d_a_up,
           d_k_k, d_k_a, d_r_k, d_gn_g, d_gn_b, w_branch, w_out, final_g):
    b, s, d = x.shape
    depth = norm_g.shape[0]
    t = b * s
    x2 = x.reshape(t, d)
    cos, sin = _rope_tables(s)
    lb_all = jnp.cumsum(jax.nn.softmax(b_lb_logits.astype(F32), axis=0), axis=0)
    lb_all = lb_all - lb_all[0:1]
    tm_in = next(m for m in (2048, 1024, 512, 256) if t % m == 0)
    for l in range(depth):
        p = _inproj(x2, norm_g[l][None, :], _pad_w_in(w_in[l]), tm_in, 512)
        oa = _band_attention(p, _band_bias(a_rel_bias[l]), b, s)
        lb = lb_all[l][None, :]
        ob = _hgrn2(p, jnp.log(lb), jnp.log1p(-lb), 1.0 - lb, b_norm_g[l][None, :], b, s)
        qr, kr, vb, qir, kir = _rope_prep(p, cos, sin, b, s)
        oc = _dsa(p, qr, kr, vb, qir, kir, b, s)
        zeros = jnp.zeros((LORA, WIDTH), F32)
        w2 = jnp.concatenate([jnp.concatenate([d_w_up[l], zeros], axis=1),
                              jnp.concatenate([zeros, d_a_up[l]], axis=1)], axis=0).astype(BF16)
        r, lw, kt, v, kk, a = _dprep(p, d_mu[l][None, :], d_w0[l][None, :], w2, d_a0[l][None, :],
                                     d_k_k[l][None, :], d_k_a[l][None, :], b, s)
        od = _rwkv7(r, lw, kt, v, kk, a, d_r_k[l].reshape(1, WIDTH), d_gn_g[l][None, :], d_gn_b[l][None, :], b, s)
        x2 = _merge(oa, ob, oc, od, p, x2, w_branch[l].astype(BF16), w_out[l].astype(BF16),
                    final_g[None, :], l == depth - 1)
    return x2.reshape(b, s, d)
```

```python
import functools
import math

import numpy as np
import jax
import jax.numpy as jnp
from jax import lax
from jax.experimental import pallas as pl
from jax.experimental.pallas import tpu as pltpu

F32 = jnp.float32
BF16 = jnp.bfloat16
HI = lax.Precision.HIGHEST

D_MODEL = 1024
HEADS = 4
HEAD_DIM = 64
WIDTH = HEADS * HEAD_DIM
CHUNK = 64
A_LEFT_CHUNKS = 8
REL_CLIP = 128
IDX_HEADS = 8
IDX_DIM = 64
TOPK_MAX = 256
LORA = 64
ROPE_THETA = 10000.0
EPS = 1e-6
GN_EPS = 64e-5
NEG = -1e30
INT_MIN = -2 ** 31
I16 = jnp.int16
I16_MIN = -2 ** 15

COL_A = 0
COL_B = 1024
COL_C = 2048
COL_CQI = 3072
COL_CKW = 3584
COL_D = 3840
COL_DWA = 4608
COL_DZ = 4864
COL_G = 5120
NP = 9216
VMEM_LIMIT = 56 * 1024 * 1024

_NT = (((1,), (1,)), ((), ()))


def _sigmoid(x):
    return 1.0 / (1.0 + jnp.exp(-x))


def _dot(a, b, precision=None):
    return jnp.dot(a, b, preferred_element_type=F32, precision=precision)


def _dot_nt(a, b, precision=None):
    return lax.dot_general(a, b, _NT, preferred_element_type=F32, precision=precision)


def _dot3(a, b, left):
    x = a if left else b
    hi = x.astype(BF16)
    rest = x - hi.astype(F32)
    mid = rest.astype(BF16)
    lo = (rest - mid.astype(F32)).astype(BF16)
    if left:
        return _dot(hi, b) + _dot(mid, b) + _dot(lo, b)
    return _dot(a, hi) + _dot(a, mid) + _dot(a, lo)


def _cparams(sem):
    return pltpu.CompilerParams(dimension_semantics=sem, vmem_limit_bytes=VMEM_LIMIT)


def _inproj_kernel(x_ref, g_ref, w_ref, o_ref, h_ref):
    @pl.when(pl.program_id(1) == 0)
    def _():
        x = x_ref[...]
        inv = lax.rsqrt(jnp.mean(x * x, axis=-1, keepdims=True) + EPS)
        h_ref[...] = ((x * inv) * g_ref[...]).astype(BF16)

    o_ref[...] = _dot(h_ref[...], w_ref[...])


def _inproj(x2, g, w_pad, tm, tn):
    t = x2.shape[0]
    return pl.pallas_call(
        _inproj_kernel,
        grid=(t // tm, NP // tn),
        in_specs=[
            pl.BlockSpec((tm, D_MODEL), lambda i, j: (i, 0)),
            pl.BlockSpec((1, D_MODEL), lambda i, j: (0, 0)),
            pl.BlockSpec((D_MODEL, tn), lambda i, j: (0, j)),
        ],
        out_specs=pl.BlockSpec((tm, tn), lambda i, j: (i, j)),
        out_shape=jax.ShapeDtypeStruct((t, NP), F32),
        scratch_shapes=[pltpu.VMEM((tm, D_MODEL), BF16)],
        compiler_params=_cparams(("parallel", "arbitrary")),
        name="inproj",
    )(x2, g, w_pad)


TQ_A = 256
WIN_A = 3 * TQ_A


def _band_kernel(q_ref, k0_ref, k1_ref, k2_ref, v0_ref, v1_ref, v2_ref, bias_ref, o_ref):
    i = pl.program_id(1)
    q = q_ref[...]
    kc = jnp.concatenate([k0_ref[...], k1_ref[...], k2_ref[...]], axis=0).astype(BF16)
    vc = jnp.concatenate([v0_ref[...], v1_ref[...], v2_ref[...]], axis=0).astype(BF16)
    lane = lax.broadcasted_iota(jnp.int32, (1, WIDTH), 1)
    kpos = lax.broadcasted_iota(jnp.int32, (1, WIN_A), 1) + (i - 2) * TQ_A
    kvalid = kpos >= 0
    out = jnp.zeros((TQ_A, WIDTH), F32)
    for h in range(HEADS):
        hm = (lane // HEAD_DIM) == h
        qh = jnp.where(hm, q, 0.0).astype(BF16)
        s = _dot_nt(qh, kc) * (HEAD_DIM ** -0.5) + bias_ref[h]
        s = jnp.where(kvalid, s, NEG)
        m = jnp.max(s, axis=-1, keepdims=True)
        p = jnp.exp(s - m)
        l = jnp.sum(p, axis=-1, keepdims=True)
        o = _dot(p.astype(BF16), vc) / l
        out = jnp.where(hm, o, out)
    o_ref[...] = out


def _band_attention(p, bias, b, s):
    nt = s // TQ_A
    qcol, kcol, vcol = COL_A // WIDTH, COL_A // WIDTH + 1, COL_A // WIDTH + 2

    def kv_spec(col, back):
        return pl.BlockSpec((TQ_A, WIDTH), lambda bb, i: (bb * nt + jnp.maximum(i - back, 0), col))

    return pl.pallas_call(
        _band_kernel,
        grid=(b, nt),
        in_specs=[
            pl.BlockSpec((TQ_A, WIDTH), lambda bb, i: (bb * nt + i, qcol)),
            kv_spec(kcol, 2), kv_spec(kcol, 1), kv_spec(kcol, 0),
            kv_spec(vcol, 2), kv_spec(vcol, 1), kv_spec(vcol, 0),
            pl.BlockSpec((HEADS, TQ_A, WIN_A), lambda bb, i: (0, 0, 0)),
        ],
        out_specs=pl.BlockSpec((TQ_A, WIDTH), lambda bb, i: (bb * nt + i, 0)),
        out_shape=jax.ShapeDtypeStruct((b * s, WIDTH), F32),
        compiler_params=_cparams(("parallel", "parallel")),
        name="band_attention",
    )(p, p, p, p, p, p, p, bias)


def _band_bias(rel_bias):
    tab = rel_bias.astype(F32)
    n = tab.shape[1]
    d_lo, d_hi = 2 * TQ_A - (WIN_A - 1), 2 * TQ_A + TQ_A - 1
    n_lo = -(CHUNK - 1) - d_lo + 1
    n_hi = d_hi - REL_CLIP + 1
    v = jnp.concatenate([jnp.repeat(tab[:, :1], n_lo, axis=1), tab[:, 1:n - 1],
                         jnp.repeat(tab[:, n - 1:], n_hi, axis=1)], axis=1)
    span = WIN_A + TQ_A
    u = jnp.concatenate([v[:, :WIN_A][:, ::-1], jnp.zeros((HEADS, span - v.shape[1]), F32),
                         v[:, WIN_A:][:, ::-1]], axis=1)
    skew = jnp.tile(u, (1, TQ_A))[:, :TQ_A * (span - 1)].reshape(HEADS, TQ_A, span - 1)[:, :, :WIN_A]
    ql = np.arange(TQ_A)[:, None]
    kl = np.arange(WIN_A)[None, :]
    lo = CHUNK * (ql // CHUNK)
    band = (kl >= lo) & (kl < lo + (A_LEFT_CHUNKS + 1) * CHUNK)
    return jnp.where(band[None], skew, NEG)


TS_B = 256
_LEVELS = (1, 2, 4, 8, 16, 32)
assert TS_B == WIDTH and CHUNK == HEAD_DIM


def _hgrn_kernel(q_ref, f_ref, i_ref, la_ref, l1_ref, oml_ref, ng_ref, o_ref, st_ref):
    @pl.when(pl.program_id(1) == 0)
    def _():
        st_ref[...] = jnp.zeros(st_ref.shape, F32)

    fl = f_ref[...]
    logsig = jnp.minimum(fl, 0.0) - jnp.log1p(jnp.exp(-jnp.abs(fl)))
    y = l1_ref[...] + logsig
    la = la_ref[...]
    g = jnp.maximum(la, y) + jnp.log1p(jnp.exp(-jnp.abs(la - y)))
    kk = oml_ref[...] * _sigmoid(-fl)
    qq = q_ref[...]
    qf = qq * _sigmoid(qq)
    vv = i_ref[...]
    tt = lax.broadcasted_iota(jnp.int32, (TS_B, TS_B), 0)
    ss = lax.broadcasted_iota(jnp.int32, (TS_B, TS_B), 1)
    same = (tt // CHUNK) == (ss // CHUNK)
    ones_bd = jnp.where(same, 1.0, 0.0).astype(BF16)
    ltri_bd = jnp.where(same & (tt >= ss), 1.0, 0.0).astype(BF16)
    hid = lax.broadcasted_iota(jnp.int32, (1, WIDTH), 1) // HEAD_DIM
    heads4 = lambda x: jnp.concatenate([jnp.where(hid == h, x, 0.0) for h in range(HEADS)], axis=0).astype(BF16)
    tile4 = lambda mask: jnp.concatenate([mask] * HEADS, axis=0)

    b = _dot3(ltri_bd, g, left=False)
    bl = _dot3(ones_bd, g, left=False)
    att4 = jnp.where(tile4(tt == ss), _dot_nt(heads4(qf), kk.astype(BF16)), 0.0)
    for m in _LEVELS:
        pick = jnp.where(ss == (tt // (2 * m)) * (2 * m) + (m - 1), 1.0, 0.0).astype(BF16)
        b_hi = b.astype(BF16)
        br = _dot(pick, b_hi) + _dot(pick, (b - b_hi.astype(F32)).astype(BF16))
        ql = qf * jnp.exp(jnp.minimum(b - br, 1.0))
        kl = kk * jnp.exp(jnp.minimum(br - b, 1.0))
        mask = ((tt // (2 * m)) == (ss // (2 * m))) & ((tt % (2 * m)) >= m) & ((ss % (2 * m)) < m)
        att4 = att4 + jnp.where(tile4(mask), _dot_nt(heads4(ql), kl.astype(BF16)), 0.0)
    att4 = att4.astype(BF16)
    o = jnp.zeros((TS_B, WIDTH), F32)
    for h in range(HEADS):
        o = o + _dot(att4[h * TS_B:(h + 1) * TS_B], jnp.where(hid == h, vv, 0.0).astype(BF16))
    qe = (qf * jnp.exp(b)).astype(BF16)
    kd = (kk * jnp.exp(bl - b)).astype(BF16)
    outs = []
    for c in range(TS_B // CHUNK):
        rows = slice(c * CHUNK, (c + 1) * CHUNK)
        st = st_ref[...]
        outs.append(_dot_nt(qe[rows], st.astype(BF16)))
        upd = _dot(vv[rows].T.astype(BF16), kd[rows])
        st_ref[...] = st * jnp.exp(bl[c * CHUNK:c * CHUNK + 1, :]) + jnp.where(same, upd, 0.0)
    o = o + jnp.concatenate(outs, axis=0)
    ms = _dot3(o * o, ones_bd, left=True) * (1.0 / HEAD_DIM)
    o_ref[...] = o * lax.rsqrt(ms + EPS) * ng_ref[...]


def _hgrn2(p, la, l1, oml, ng, b, s):
    nt = s // TS_B
    c = COL_B // WIDTH
    row = lambda col: pl.BlockSpec((TS_B, WIDTH), lambda bb, i: (bb * nt + i, col))
    vec = pl.BlockSpec((1, WIDTH), lambda bb, i: (0, 0))
    return pl.pallas_call(
        _hgrn_kernel,
        grid=(b, nt),
        in_specs=[row(c), row(c + 1), row(c + 2), vec, vec, vec, vec],
        out_specs=pl.BlockSpec((TS_B, WIDTH), lambda bb, i: (bb * nt + i, 0)),
        out_shape=jax.ShapeDtypeStruct((b * s, WIDTH), F32),
        scratch_shapes=[pltpu.VMEM((WIDTH, WIDTH), F32)],
        compiler_params=_cparams(("parallel", "arbitrary")),
        name="hgrn2",
    )(p, p, p, la, l1, oml, jnp.tile(ng, (1, HEADS)))


TM_R = 256
TQ_C = 256
KB_C = 512


def _rope(x, cos, sin_signed):
    w = x.shape[-1]
    lane = lax.broadcasted_iota(jnp.int32, x.shape, 1)
    low = (lane % HEAD_DIM) < (HEAD_DIM // 2)
    swapped = jnp.where(low, pltpu.roll(x, w - HEAD_DIM // 2, 1), pltpu.roll(x, HEAD_DIM // 2, 1))
    return x * cos + swapped * sin_signed


def _rope_kernel(q_ref, k_ref, v_ref, qi_ref, kw_ref, cos_ref, sin_ref, qo_ref, ko_ref, vo_ref, qio_ref, kio_ref):
    cos = cos_ref[...]
    sin = sin_ref[...]
    qo_ref[...] = _rope(q_ref[...], cos, sin).astype(BF16)
    ko_ref[...] = _rope(k_ref[...], cos, sin).astype(BF16)
    vo_ref[...] = v_ref[...].astype(BF16)
    cos2 = jnp.concatenate([cos, cos], axis=1)
    sin2 = jnp.concatenate([sin, sin], axis=1)
    qio_ref[...] = _rope(qi_ref[...], cos2, sin2).astype(BF16)
    kr = _rope(kw_ref[...], cos[:, :128], sin[:, :128])
    lane = lax.broadcasted_iota(jnp.int32, kr.shape, 1)
    kio_ref[...] = jnp.where(lane < IDX_DIM, kr, pltpu.roll(kr, IDX_DIM, 1)).astype(BF16)


def _rope_prep(p, cos, sin, b, s):
    t = b * s
    nt = s // TM_R
    c = COL_C // WIDTH
    row = lambda w, col: pl.BlockSpec((TM_R, w), lambda i: (i, col))
    tab = pl.BlockSpec((TM_R, WIDTH), lambda i: (i % nt, 0))
    return pl.pallas_call(
        _rope_kernel,
        grid=(t // TM_R,),
        in_specs=[row(WIDTH, c), row(WIDTH, c + 1), row(WIDTH, c + 2),
                  row(512, COL_CQI // 512), row(128, COL_CKW // 128), tab, tab],
        out_specs=[row(WIDTH, 0), row(WIDTH, 0), row(WIDTH, 0), row(512, 0), row(128, 0)],
        out_shape=[jax.ShapeDtypeStruct((t, WIDTH), BF16)] * 3
        + [jax.ShapeDtypeStruct((t, 512), BF16), jax.ShapeDtypeStruct((t, 128), BF16)],
        compiler_params=_cparams(("parallel",)),
        name="rope_prep",
    )(p, p, p, p, p, cos, sin)


def _rope_tables(s):
    half = HEAD_DIM // 2
    freqs = ROPE_THETA ** (-jnp.arange(half, dtype=F32) / half)
    ang = jnp.arange(s, dtype=jnp.int32).astype(F32)[:, None] * freqs[None, :]
    cos = jnp.cos(ang)
    sin = jnp.sin(ang)
    cos = jnp.tile(jnp.concatenate([cos, cos], axis=1), (1, HEADS))
    sin = jnp.tile(jnp.concatenate([-sin, sin], axis=1), (1, HEADS))
    return cos, sin


def _dsa_kernel(q_ref, k_ref, v_ref, qi_ref, ki_ref, w_ref, o_ref, hi_ref, lo_ref, *, topk, idx_bits):
    i = pl.program_id(1)
    nkb = (i * TQ_C) // KB_C + 1
    lane128 = lax.broadcasted_iota(jnp.int32, (1, 128), 1)
    lane256 = lax.broadcasted_iota(jnp.int32, (1, WIDTH), 1)
    kiota = lax.broadcasted_iota(jnp.int32, (1, KB_C), 1)
    qchunk = (i * TQ_C + lax.broadcasted_iota(jnp.int32, (TQ_C, 1), 0)) // CHUNK
    w = w_ref[...] * (IDX_HEADS ** -0.5)
    qi = qi_ref[...] * jnp.asarray(IDX_DIM ** -0.5, BF16)
    qi_heads = [jnp.where((lane128 // IDX_DIM) == (h % 2), qi[:, 128 * (h // 2):128 * (h // 2 + 1)],
                          jnp.zeros((), BF16)) for h in range(IDX_HEADS)]

    def score_block(kb, carry):
        off = pl.multiple_of(kb * KB_C, KB_C)
        kib = ki_ref[pl.ds(off, KB_C), :]
        acc = jnp.zeros((TQ_C, KB_C), F32)
        for h in range(IDX_HEADS):
            acc = acc + jnp.maximum(_dot_nt(qi_heads[h], kib), 0.0) * w[:, IDX_DIM + h:IDX_DIM + h + 1]
        bits = lax.bitcast_convert_type(acc, jnp.int32)
        key = bits ^ ((bits >> 31) & jnp.int32(0x7FFFFFFF))
        key = jnp.where(acc == 0.0, 0, key)
        adm = ((off + kiota) // CHUNK) <= qchunk
        key = jnp.where(adm, key, INT_MIN)
        hi_ref[kb] = (key >> 16).astype(I16)
        lo_ref[kb] = ((key & 0xFFFF) - 32768).astype(I16)
        return carry

    lax.fori_loop(0, nkb, score_block, 0)

    one16 = jnp.ones((), I16)
    zero16 = jnp.zeros((), I16)

    def count(pred):
        def body(kb, acc):
            c = jnp.where(pred(kb), one16, zero16)
            return acc + (c[:, 0:128] + c[:, 128:256] + c[:, 256:384] + c[:, 384:512])
        acc = lax.fori_loop(0, nkb, body, jnp.zeros((TQ_C, 128), I16))
        return jnp.sum(acc.astype(F32), axis=1, keepdims=True)

    def descend(ref, kth):
        def bit(bi, t):
            cand = t + lax.shift_left(jnp.int32(1), 15 - bi)
            c16 = cand.astype(I16)
            return jnp.where(count(lambda kb: ref[kb] >= c16) >= kth, cand, t)
        return lax.fori_loop(0, 16, bit, jnp.full((TQ_C, 1), I16_MIN, jnp.int32))

    tau_hi32 = descend(hi_ref, topk)
    tau_hi = tau_hi32.astype(I16)
    kth_lo = topk - count(lambda kb: hi_ref[kb] > tau_hi)

    def mask_low(kb, carry):
        lo_ref[kb] = jnp.where(hi_ref[kb] == tau_hi, lo_ref[kb], jnp.asarray(I16_MIN, I16))
        return carry

    lax.fori_loop(0, nkb, mask_low, 0)
    tau_lo = descend(lo_ref, kth_lo).astype(I16)
    need = kth_lo - count(lambda kb: lo_ref[kb] > tau_lo)
    real = tau_hi32 > I16_MIN
    real16 = tau_hi > jnp.asarray(I16_MIN, I16)

    def is_tie(kb):
        return (hi_ref[kb] == tau_hi) & (lo_ref[kb] == tau_lo)

    ties = count(is_tie)

    def index_search():
        def index_bit(bi, jmax):
            cand = jmax + lax.shift_left(jnp.int32(1), idx_bits - 1 - bi)
            c16 = cand.astype(I16)
            cnt = count(lambda kb: is_tie(kb) & ((kb * KB_C + kiota).astype(I16) < c16))
            return jnp.where(cnt < need, cand, jmax)
        return lax.fori_loop(0, idx_bits, index_bit, jnp.zeros((TQ_C, 1), jnp.int32))

    excess = jnp.max(jnp.where((ties > need) & real, 1.0, 0.0)) > 0.5
    jmax = lax.cond(excess, index_search, lambda: jnp.full((TQ_C, 1), 2 ** (idx_bits - 1), jnp.int32))
    jmax = jmax.astype(I16)

    q = q_ref[...] * jnp.asarray(HEAD_DIM ** -0.5, BF16)
    q4 = jnp.concatenate([jnp.where((lane256 // HEAD_DIM) == h, q, jnp.zeros((), BF16)) for h in range(HEADS)],
                         axis=0)

    def attend_block(kb, carry):
        m_old, l_old, acc_old = carry
        off = pl.multiple_of(kb * KB_C, KB_C)
        kblk = k_ref[pl.ds(off, KB_C), :]
        vblk = v_ref[pl.ds(off, KB_C), :]
        hi = hi_ref[kb]
        lo = lo_ref[kb]
        in_order = (off + kiota).astype(I16) <= jmax
        sel = (hi > tau_hi) | ((hi == tau_hi) & real16 & ((lo > tau_lo) | ((lo == tau_lo) & in_order)))
        madd = (jnp.where(sel, one16, zero16).astype(F32) - 1.0) * (-NEG)
        s = jnp.concatenate([_dot_nt(q4, kblk[:KB_C // 2]), _dot_nt(q4, kblk[KB_C // 2:])], axis=1)
        s = s + jnp.concatenate([madd] * HEADS, axis=0)
        m_new = jnp.maximum(m_old, jnp.max(s, axis=-1, keepdims=True))
        p = jnp.exp(s - m_new)
        corr = jnp.exp(m_old - m_new)
        l_new = corr * l_old + jnp.sum(p, axis=-1, keepdims=True)
        pb = p.astype(BF16)
        pv = jnp.concatenate([_dot(pb, vblk[:, :WIDTH // 2]), _dot(pb, vblk[:, WIDTH // 2:])], axis=1)
        return m_new, l_new, corr * acc_old + pv

    rows = HEADS * TQ_C
    init = (jnp.full((rows, 1), NEG, F32), jnp.zeros((rows, 1), F32), jnp.zeros((rows, WIDTH), F32))
    _, l_fin, acc_fin = lax.fori_loop(0, nkb, attend_block, init)
    out = jnp.zeros((TQ_C, WIDTH), F32)
    for h in range(HEADS):
        hr = slice(h * TQ_C, (h + 1) * TQ_C)
        out = jnp.where((lane256 // HEAD_DIM) == h, acc_fin[hr] / l_fin[hr], out)
    o_ref[...] = out


def _dsa(p, qr, kr, vb, qir, kir, b, s):
    nt = s // TQ_C
    topk = min(TOPK_MAX, s // 4)
    idx_bits = int(math.log2(s)) + 1
    qrow = lambda w, col: pl.BlockSpec((TQ_C, w), lambda bb, i: (bb * nt + i, col))
    full = lambda w: pl.BlockSpec((s, w), lambda bb, i: (bb, 0))
    return pl.pallas_call(
        functools.partial(_dsa_kernel, topk=topk, idx_bits=idx_bits),
        grid=(b, nt),
        in_specs=[qrow(WIDTH, 0), full(WIDTH), full(WIDTH), qrow(512, 0), full(128),
                  qrow(128, COL_CKW // 128)],
        out_specs=qrow(WIDTH, 0),
        out_shape=jax.ShapeDtypeStruct((b * s, WIDTH), F32),
        scratch_shapes=[
            pltpu.VMEM((s // KB_C, TQ_C, KB_C), I16),
            pltpu.VMEM((s // KB_C, TQ_C, KB_C), I16),
        ],
        compiler_params=_cparams(("parallel", "arbitrary")),
        name="dsa",
    )(qr, kr, vb, qir, kir, p)


TM_D = 256
TS_D = 256
assert TS_D == WIDTH and CHUNK == HEAD_DIM


def _dprep_kernel(r_ref, k_ref, v_ref, wa_ref, rp_ref, kp_ref, vp_ref, wap_ref,
                  mur_ref, muk_ref, muv_ref, muwa_ref, w0_ref, w2_ref, a0_ref, kkk_ref, ka_ref,
                  ro_ref, lwo_ref, kto_ref, vo_ref, kko_ref, ao_ref, *, seq):
    first = (pl.program_id(0) * TM_D) % seq == 0

    def shift(x_ref, p_ref, mu_ref):
        x = x_ref[...]
        row0 = jnp.where(first, 0.0, p_ref[7:8, :])
        rid = lax.broadcasted_iota(jnp.int32, x.shape, 0)
        prev = jnp.where(rid == 0, row0, pltpu.roll(x, 1, 0))
        return x + mu_ref[...] * (prev - x)

    r = shift(r_ref, rp_ref, mur_ref)
    k = shift(k_ref, kp_ref, muk_ref)
    v = shift(v_ref, vp_ref, muv_ref)
    wa = shift(wa_ref, wap_ref, muwa_ref)
    lane = lax.broadcasted_iota(jnp.int32, wa.shape, 1)
    lora_in = jnp.where(lane < LORA, jnp.tanh(wa), wa).astype(BF16)
    lora = _dot(lora_in, w2_ref[...])
    z = w0_ref[...] + lora[:, :WIDTH]
    wlog = -(jnp.maximum(-z, 0.0) + jnp.log1p(jnp.exp(-jnp.abs(z)))) - 0.5
    a = _sigmoid(a0_ref[...] + lora[:, WIDTH:])
    ro_ref[...] = r
    lwo_ref[...] = -jnp.exp(wlog)
    kto_ref[...] = k * (1.0 + (a - 1.0) * ka_ref[...])
    vo_ref[...] = v
    kko_ref[...] = k * kkk_ref[...]
    ao_ref[...] = a


def _dprep(p, mu, w0, w2, a0, kkk, ka, b, s):
    t = b * s
    rb = TM_D // 8
    c = COL_D // WIDTH
    row = lambda w, col: pl.BlockSpec((TM_D, w), lambda i: (i, col))
    prev = lambda w, col: pl.BlockSpec((8, w), lambda i: (jnp.maximum(i * rb - 1, 0), col))
    vec = lambda w: pl.BlockSpec((1, w), lambda i: (0, 0))
    mur, muk, muv = mu[:, 0:256], mu[:, 256:512], mu[:, 512:768]
    muwa = mu[:, 768:896]
    return pl.pallas_call(
        functools.partial(_dprep_kernel, seq=s),
        grid=(t // TM_D,),
        in_specs=[row(WIDTH, c), row(WIDTH, c + 1), row(WIDTH, c + 2), row(128, COL_DWA // 128),
                  prev(WIDTH, c), prev(WIDTH, c + 1), prev(WIDTH, c + 2), prev(128, COL_DWA // 128),
                  vec(WIDTH), vec(WIDTH), vec(WIDTH), vec(128), vec(WIDTH),
                  pl.BlockSpec((128, 2 * WIDTH), lambda i: (0, 0)), vec(WIDTH), vec(WIDTH), vec(WIDTH)],
        out_specs=[row(WIDTH, 0)] * 6,
        out_shape=[jax.ShapeDtypeStruct((t, WIDTH), F32)] * 6,
        compiler_params=_cparams(("parallel",)),
        name="rwkv_prep",
    )(p, p, p, p, p, p, p, p, mur, muk, muv, muwa, w0, w2, a0, kkk, ka)


def _rwkv_kernel(r_ref, lw_ref, kt_ref, v_ref, kk_ref, a_ref, rk_ref, gg_ref, gb_ref, o_ref, st_ref):
    @pl.when(pl.program_id(1) == 0)
    def _():
        st_ref[...] = jnp.zeros(st_ref.shape, F32)

    tt = lax.broadcasted_iota(jnp.int32, (TS_D, TS_D), 0)
    ss = lax.broadcasted_iota(jnp.int32, (TS_D, TS_D), 1)
    same = (tt // CHUNK) == (ss // CHUNK)
    strict = same & (tt > ss)
    incl = same & (tt >= ss)
    eye = jnp.where(tt == ss, 1.0, 0.0)
    ones_bd = jnp.where(same, 1.0, 0.0).astype(BF16)
    ltri_bd = jnp.where(incl, 1.0, 0.0).astype(BF16)
    hid = lax.broadcasted_iota(jnp.int32, (1, WIDTH), 1) // HEAD_DIM

    def group_sum(x):
        return _dot3(x, ones_bd, left=True)

    rr, lw, kt, vv, kkr, aa = (x[...] for x in (r_ref, lw_ref, kt_ref, v_ref, kk_ref, a_ref))
    kk = kkr / jnp.maximum(jnp.sqrt(group_sum(kkr * kkr)), 1e-12)
    be = kk * aa
    b = _dot3(ltri_bd, lw, left=False)
    bl = _dot3(ones_bd, lw, left=False)
    eb = jnp.exp(b)
    enb = jnp.exp(-b)
    dec = jnp.exp(bl - b)
    qa = -kk * jnp.exp(b - lw)
    qr = rr * eb
    q2 = jnp.concatenate([qa, qr], axis=0)
    k2 = jnp.concatenate([be * enb, kt * enb], axis=0).astype(BF16)
    zero = jnp.zeros((TS_D, WIDTH), F32)
    sa0, qt, qo, o0 = zero, zero, zero, zero
    for h in range(HEADS):
        hm = hid == h
        g = _dot_nt(jnp.where(hm, q2, 0.0).astype(BF16), k2)
        a_ab = jnp.where(strict, g[:TS_D, :TS_D], 0.0)
        a_ak = jnp.where(strict, g[:TS_D, TS_D:], 0.0).astype(BF16)
        a_rb = jnp.where(incl, g[TS_D:, :TS_D], 0.0).astype(BF16)
        a_rk = jnp.where(incl, g[TS_D:, TS_D:], 0.0).astype(BF16)
        tm = eye + a_ab
        pw = a_ab.astype(BF16)
        for _ in range(int(math.log2(CHUNK)) - 1):
            pwf = _dot(pw, pw)
            pw = pwf.astype(BF16)
            tm = tm + _dot(tm.astype(BF16), pw)
        tmb = tm.astype(BF16)
        vh = jnp.where(hm, vv, 0.0).astype(BF16)
        sa0_h = _dot(tmb, _dot(a_ak, vh).astype(BF16))
        qt_h = _dot(tmb, jnp.where(hm, qa, 0.0).astype(BF16))
        sa0 = sa0 + sa0_h
        qt = qt + qt_h
        qo = qo + jnp.where(hm, qr, 0.0) + _dot(a_rb, qt_h.astype(BF16))
        o0 = o0 + _dot(a_rb, sa0_h.astype(BF16)) + _dot(a_rk, vh)
    bd = be * dec
    kd = kt * dec
    outs = []
    for c in range(TS_D // CHUNK):
        rows = slice(c * CHUNK, (c + 1) * CHUNK)
        st = st_ref[...]
        stb = st.astype(BF16)
        sa_c = sa0[rows] + _dot_nt(qt[rows].astype(BF16), stb)
        outs.append(o0[rows] + _dot_nt(qo[rows].astype(BF16), stb))
        lhs_t = jnp.concatenate([sa_c, vv[rows]], axis=0).T.astype(BF16)
        rhs = jnp.concatenate([bd[rows], kd[rows]], axis=0).astype(BF16)
        upd = _dot(lhs_t, rhs)
        st_ref[...] = st * jnp.exp(bl[c * CHUNK:c * CHUNK + 1, :]) + jnp.where(same, upd, 0.0)
    o = jnp.concatenate(outs, axis=0)
    mean = group_sum(o) * (1.0 / HEAD_DIM)
    cen = o - mean
    var = group_sum(cen * cen) * (1.0 / HEAD_DIM)
    o = cen * lax.rsqrt(var + GN_EPS) * gg_ref[...] + gb_ref[...]
    o_ref[...] = o + group_sum(rr * kt * rk_ref[...]) * vv


def _rwkv7(r, lw, kt, v, kk, a, rk, gg, gb, b, s):
    nt = s // TS_D
    row = pl.BlockSpec((TS_D, WIDTH), lambda bb, i: (bb * nt + i, 0))
    vec = pl.BlockSpec((1, WIDTH), lambda bb, i: (0, 0))
    return pl.pallas_call(
        _rwkv_kernel,
        grid=(b, nt),
        in_specs=[row] * 6 + [vec] * 3,
        out_specs=row,
        out_shape=jax.ShapeDtypeStruct((b * s, WIDTH), F32),
        scratch_shapes=[pltpu.VMEM((WIDTH, WIDTH), F32)],
        compiler_params=_cparams(("parallel", "arbitrary")),
        name="rwkv7",
    )(r, lw, kt, v, kk, a, rk, gg, gb)


TM_M = 256


def _merge_kernel(oa_ref, ob_ref, oc_ref, od_ref, za_ref, zb_ref, zc_ref, zd_ref,
                  ga_ref, gb_ref, gc_ref, gd_ref, x_ref, wb_ref, wo_ref, fg_ref, o_ref, *, final):
    def branch(o_r, z_r, g_r, idx):
        z = z_r[...]
        u = (o_r[...] * (z * _sigmoid(z))).astype(BF16)
        return _sigmoid(g_r[...]) * _dot(u, wb_ref[idx])

    y = (branch(oa_ref, za_ref, ga_ref, 0) + branch(ob_ref, zb_ref, gb_ref, 1)
         + branch(oc_ref, zc_ref, gc_ref, 2) + branch(od_ref, zd_ref, gd_ref, 3))
    xn = x_ref[...] + _dot(y.astype(BF16), wo_ref[...])
    if final:
        inv = lax.rsqrt(jnp.mean(xn * xn, axis=-1, keepdims=True) + EPS)
        xn = (xn * inv) * fg_ref[...]
    o_ref[...] = xn


def _merge(oa, ob, oc, od, p, x2, wb, wo, fg, final):
    t = x2.shape[0]
    row = lambda w, col: pl.BlockSpec((TM_M, w), lambda i: (i, col))
    zcol = lambda base: (base + 3 * WIDTH) // WIDTH
    gcol = lambda n: COL_G // D_MODEL + n
    return pl.pallas_call(
        functools.partial(_merge_kernel, final=final),
        grid=(t // TM_M,),
        in_specs=[row(WIDTH, 0)] * 4
        + [row(WIDTH, zcol(COL_A)), row(WIDTH, zcol(COL_B)), row(WIDTH, zcol(COL_C)), row(WIDTH, COL_DZ // WIDTH)]
        + [row(D_MODEL, gcol(n)) for n in range(4)]
        + [row(D_MODEL, 0),
           pl.BlockSpec((4, WIDTH, D_MODEL), lambda i: (0, 0, 0)),
           pl.BlockSpec((D_MODEL, D_MODEL), lambda i: (0, 0)),
           pl.BlockSpec((1, D_MODEL), lambda i: (0, 0))],
        out_specs=row(D_MODEL, 0),
        out_shape=jax.ShapeDtypeStruct((t, D_MODEL), F32),
        compiler_params=_cparams(("parallel",)),
        name="merge",
    )(oa, ob, oc, od, p, p, p, p, p, p, p, p, x2, wb, wo, fg)


def _pad_w_in(w):
    def seg(lo, hi, width):
        s = w[:, lo:hi]
        return jnp.pad(s, ((0, 0), (0, width - (hi - lo))))
    c_end = 2048 + 1608
    d_end = c_end + 896
    parts = [w[:, :2048], seg(2048, c_end, COL_D - COL_C), seg(c_end, d_end, COL_DZ - COL_D), w[:, d_end:]]
    return jnp.concatenate(parts, axis=1).astype(BF16)


def kernel(x, norm_g, w_in, a_rel_bias, b_lb_logits, b_norm_g, d_mu, d_w0, d_w_up, d_a0, d_a_up,
           d_k_k, d_k_a, d_r_k, d_gn_g, d_gn_b, w_branch, w_out, final_g):
    b, s, d = x.shape
    depth = norm_g.shape[0]
    t = b * s
    x2 = x.reshape(t, d)
    cos, sin = _rope_tables(s)
    lb_all = jnp.cumsum(jax.nn.softmax(b_lb_logits.astype(F32), axis=0), axis=0)
    lb_all = lb_all - lb_all[0:1]
    tm_in = next(m for m in (2048, 1024, 512, 256) if t % m == 0)
    for l in range(depth):
        p = _inproj(x2, norm_g[l][None, :], _pad_w_in(w_in[l]), tm_in, 512)
        oa = _band_attention(p, _band_bias(a_rel_bias[l]), b, s)
        lb = lb_all[l][None, :]
        ob = _hgrn2(p, jnp.log(lb), jnp.log1p(-lb), 1.0 - lb, b_norm_g[l][None, :], b, s)
        qr, kr, vb, qir, kir = _rope_prep(p, cos, sin, b, s)
        oc = _dsa(p, qr, kr, vb, qir, kir, b, s)
        zeros = jnp.zeros((LORA, WIDTH), F32)
        w2 = jnp.concatenate([jnp.concatenate([d_w_up[l], zeros], axis=1),
                              jnp.concatenate([zeros, d_a_up[l]], axis=1)], axis=0).astype(BF16)
        r, lw, kt, v, kk, a = _dprep(p, d_mu[l][None, :], d_w0[l][None, :], w2, d_a0[l][None, :],
                                     d_k_k[l][None, :], d_k_a[l][None, :], b, s)
        od = _rwkv7(r, lw, kt, v, kk, a, d_r_k[l].reshape(1, WIDTH), d_gn_g[l][None, :], d_gn_b[l][None, :], b, s)
        x2 = _merge(oa, ob, oc, od, p, x2, w_branch[l].astype(BF16), w_out[l].astype(BF16),
                    final_g[None, :], l == depth - 1)
    return x2.reshape(b, s, d)
```

```python
import functools
import math

import numpy as np
import jax
import jax.numpy as jnp
from jax import lax
from jax.experimental import pallas as pl
from jax.experimental.pallas import tpu as pltpu

F32 = jnp.float32
BF16 = jnp.bfloat16
HI = lax.Precision.HIGHEST

D_MODEL = 1024
HEADS = 4
HEAD_DIM = 64
WIDTH = HEADS * HEAD_DIM
CHUNK = 64
A_LEFT_CHUNKS = 8
REL_CLIP = 128
IDX_HEADS = 8
IDX_DIM = 64
TOPK_MAX = 256
LORA = 64
ROPE_THETA = 10000.0
EPS = 1e-6
GN_EPS = 64e-5
NEG = -1e30
INT_MIN = -2 ** 31
I16 = jnp.int16
I16_MIN = -2 ** 15

COL_A = 0
COL_B = 1024
COL_C = 2048
COL_CQI = 3072
COL_CKW = 3584
COL_D = 3840
COL_DWA = 4608
COL_DZ = 4864
COL_G = 5120
NP = 9216
VMEM_LIMIT = 56 * 1024 * 1024

_NT = (((1,), (1,)), ((), ()))


def _sigmoid(x):
    return 1.0 / (1.0 + jnp.exp(-x))


def _dot(a, b, precision=None):
    return jnp.dot(a, b, preferred_element_type=F32, precision=precision)


def _dot_nt(a, b, precision=None):
    return lax.dot_general(a, b, _NT, preferred_element_type=F32, precision=precision)


def _dot3(a, b, left):
    x = a if left else b
    hi = x.astype(BF16)
    rest = x - hi.astype(F32)
    mid = rest.astype(BF16)
    lo = (rest - mid.astype(F32)).astype(BF16)
    if left:
        return _dot(hi, b) + _dot(mid, b) + _dot(lo, b)
    return _dot(a, hi) + _dot(a, mid) + _dot(a, lo)


def _cparams(sem):
    return pltpu.CompilerParams(dimension_semantics=sem, vmem_limit_bytes=VMEM_LIMIT)


def _inproj_kernel(x_ref, g_ref, w_ref, o_ref, h_ref):
    @pl.when(pl.program_id(1) == 0)
    def _():
        x = x_ref[...]
        inv = lax.rsqrt(jnp.mean(x * x, axis=-1, keepdims=True) + EPS)
        h_ref[...] = ((x * inv) * g_ref[...]).astype(BF16)

    o_ref[...] = _dot(h_ref[...], w_ref[...])


def _inproj(x2, g, w_pad, tm, tn):
    t = x2.shape[0]
    return pl.pallas_call(
        _inproj_kernel,
        grid=(t // tm, NP // tn),
        in_specs=[
            pl.BlockSpec((tm, D_MODEL), lambda i, j: (i, 0)),
            pl.BlockSpec((1, D_MODEL), lambda i, j: (0, 0)),
            pl.BlockSpec((D_MODEL, tn), lambda i, j: (0, j)),
        ],
        out_specs=pl.BlockSpec((tm, tn), lambda i, j: (i, j)),
        out_shape=jax.ShapeDtypeStruct((t, NP), F32),
        scratch_shapes=[pltpu.VMEM((tm, D_MODEL), BF16)],
        compiler_params=_cparams(("parallel", "arbitrary")),
        name="inproj",
    )(x2, g, w_pad)


TQ_A = 256
WIN_A = 3 * TQ_A


def _band_kernel(q_ref, k0_ref, k1_ref, k2_ref, v0_ref, v1_ref, v2_ref, bias_ref, o_ref):
    i = pl.program_id(1)
    q = q_ref[...]
    kc = jnp.concatenate([k0_ref[...], k1_ref[...], k2_ref[...]], axis=0).astype(BF16)
    vc = jnp.concatenate([v0_ref[...], v1_ref[...], v2_ref[...]], axis=0).astype(BF16)
    lane = lax.broadcasted_iota(jnp.int32, (1, WIDTH), 1)
    kpos = lax.broadcasted_iota(jnp.int32, (1, WIN_A), 1) + (i - 2) * TQ_A
    kvalid = kpos >= 0
    out = jnp.zeros((TQ_A, WIDTH), F32)
    for h in range(HEADS):
        hm = (lane // HEAD_DIM) == h
        qh = jnp.where(hm, q, 0.0).astype(BF16)
        s = _dot_nt(qh, kc) * (HEAD_DIM ** -0.5) + bias_ref[h]
        s = jnp.where(kvalid, s, NEG)
        m = jnp.max(s, axis=-1, keepdims=True)
        p = jnp.exp(s - m)
        l = jnp.sum(p, axis=-1, keepdims=True)
        o = _dot(p.astype(BF16), vc) / l
        out = jnp.where(hm, o, out)
    o_ref[...] = out


def _band_attention(p, bias, b, s):
    nt = s // TQ_A
    qcol, kcol, vcol = COL_A // WIDTH, COL_A // WIDTH + 1, COL_A // WIDTH + 2

    def kv_spec(col, back):
        return pl.BlockSpec((TQ_A, WIDTH), lambda bb, i: (bb * nt + jnp.maximum(i - back, 0), col))

    return pl.pallas_call(
        _band_kernel,
        grid=(b, nt),
        in_specs=[
            pl.BlockSpec((TQ_A, WIDTH), lambda bb, i: (bb * nt + i, qcol)),
            kv_spec(kcol, 2), kv_spec(kcol, 1), kv_spec(kcol, 0),
            kv_spec(vcol, 2), kv_spec(vcol, 1), kv_spec(vcol, 0),
            pl.BlockSpec((HEADS, TQ_A, WIN_A), lambda bb, i: (0, 0, 0)),
        ],
        out_specs=pl.BlockSpec((TQ_A, WIDTH), lambda bb, i: (bb * nt + i, 0)),
        out_shape=jax.ShapeDtypeStruct((b * s, WIDTH), F32),
        compiler_params=_cparams(("parallel", "parallel")),
        name="band_attention",
    )(p, p, p, p, p, p, p, bias)


def _band_bias(rel_bias):
    tab = rel_bias.astype(F32)
    n = tab.shape[1]
    d_lo, d_hi = 2 * TQ_A - (WIN_A - 1), 2 * TQ_A + TQ_A - 1
    n_lo = -(CHUNK - 1) - d_lo + 1
    n_hi = d_hi - REL_CLIP + 1
    v = jnp.concatenate([jnp.repeat(tab[:, :1], n_lo, axis=1), tab[:, 1:n - 1],
                         jnp.repeat(tab[:, n - 1:], n_hi, axis=1)], axis=1)
    span = WIN_A + TQ_A
    u = jnp.concatenate([v[:, :WIN_A][:, ::-1], jnp.zeros((HEADS, span - v.shape[1]), F32),
                         v[:, WIN_A:][:, ::-1]], axis=1)
    skew = jnp.tile(u, (1, TQ_A))[:, :TQ_A * (span - 1)].reshape(HEADS, TQ_A, span - 1)[:, :, :WIN_A]
    ql = np.arange(TQ_A)[:, None]
    kl = np.arange(WIN_A)[None, :]
    lo = CHUNK * (ql // CHUNK)
    band = (kl >= lo) & (kl < lo + (A_LEFT_CHUNKS + 1) * CHUNK)
    return jnp.where(band[None], skew, NEG)


TS_B = 256
_LEVELS = (1, 2, 4, 8, 16, 32)
assert TS_B == WIDTH and CHUNK == HEAD_DIM


def _hgrn_kernel(q_ref, f_ref, i_ref, la_ref, l1_ref, oml_ref, ng_ref, o_ref, st_ref):
    @pl.when(pl.program_id(1) == 0)
    def _():
        st_ref[...] = jnp.zeros(st_ref.shape, F32)

    fl = f_ref[...]
    logsig = jnp.minimum(fl, 0.0) - jnp.log1p(jnp.exp(-jnp.abs(fl)))
    y = l1_ref[...] + logsig
    la = la_ref[...]
    g = jnp.maximum(la, y) + jnp.log1p(jnp.exp(-jnp.abs(la - y)))
    kk = oml_ref[...] * _sigmoid(-fl)
    qq = q_ref[...]
    qf = qq * _sigmoid(qq)
    vv = i_ref[...]
    tt = lax.broadcasted_iota(jnp.int32, (TS_B, TS_B), 0)
    ss = lax.broadcasted_iota(jnp.int32, (TS_B, TS_B), 1)
    same = (tt // CHUNK) == (ss // CHUNK)
    ones_bd = jnp.where(same, 1.0, 0.0).astype(BF16)
    ltri_bd = jnp.where(same & (tt >= ss), 1.0, 0.0).astype(BF16)
    hid = lax.broadcasted_iota(jnp.int32, (1, WIDTH), 1) // HEAD_DIM
    heads4 = lambda x: jnp.concatenate([jnp.where(hid == h, x, 0.0) for h in range(HEADS)], axis=0).astype(BF16)
    tile4 = lambda mask: jnp.concatenate([mask] * HEADS, axis=0)

    b = _dot3(ltri_bd, g, left=False)
    bl = _dot3(ones_bd, g, left=False)
    att4 = jnp.where(tile4(tt == ss), _dot_nt(heads4(qf), kk.astype(BF16)), 0.0)
    for m in _LEVELS:
        pick = jnp.where(ss == (tt // (2 * m)) * (2 * m) + (m - 1), 1.0, 0.0).astype(BF16)
        b_hi = b.astype(BF16)
        br = _dot(pick, b_hi) + _dot(pick, (b - b_hi.astype(F32)).astype(BF16))
        ql = qf * jnp.exp(jnp.minimum(b - br, 1.0))
        kl = kk * jnp.exp(jnp.minimum(br - b, 1.0))
        mask = ((tt // (2 * m)) == (ss // (2 * m))) & ((tt % (2 * m)) >= m) & ((ss % (2 * m)) < m)
        att4 = att4 + jnp.where(tile4(mask), _dot_nt(heads4(ql), kl.astype(BF16)), 0.0)
    att4 = att4.astype(BF16)
    o = jnp.zeros((TS_B, WIDTH), F32)
    for h in range(HEADS):
        o = o + _dot(att4[h * TS_B:(h + 1) * TS_B], jnp.where(hid == h, vv, 0.0).astype(BF16))
    qe = (qf * jnp.exp(b)).astype(BF16)
    kd = (kk * jnp.exp(bl - b)).astype(BF16)
    outs = []
    for c in range(TS_B // CHUNK):
        rows = slice(c * CHUNK, (c + 1) * CHUNK)
        st = st_ref[...]
        outs.append(_dot_nt(qe[rows], st.astype(BF16)))
        upd = _dot(vv[rows].T.astype(BF16), kd[rows])
        st_ref[...] = st * jnp.exp(bl[c * CHUNK:c * CHUNK + 1, :]) + jnp.where(same, upd, 0.0)
    o = o + jnp.concatenate(outs, axis=0)
    ms = _dot3(o * o, ones_bd, left=True) * (1.0 / HEAD_DIM)
    o_ref[...] = o * lax.rsqrt(ms + EPS) * ng_ref[...]


def _hgrn2(p, la, l1, oml, ng, b, s):
    nt = s // TS_B
    c = COL_B // WIDTH
    row = lambda col: pl.BlockSpec((TS_B, WIDTH), lambda bb, i: (bb * nt + i, col))
    vec = pl.BlockSpec((1, WIDTH), lambda bb, i: (0, 0))
    return pl.pallas_call(
        _hgrn_kernel,
        grid=(b, nt),
        in_specs=[row(c), row(c + 1), row(c + 2), vec, vec, vec, vec],
        out_specs=pl.BlockSpec((TS_B, WIDTH), lambda bb, i: (bb * nt + i, 0)),
        out_shape=jax.ShapeDtypeStruct((b * s, WIDTH), F32),
        scratch_shapes=[pltpu.VMEM((WIDTH, WIDTH), F32)],
        compiler_params=_cparams(("parallel", "arbitrary")),
        name="hgrn2",
    )(p, p, p, la, l1, oml, jnp.tile(ng, (1, HEADS)))


TQ_C = 256
KB_C = 512
TM_R = KB_C


def _rope(x, cos, sin_signed):
    w = x.shape[-1]
    lane = lax.broadcasted_iota(jnp.int32, x.shape, 1)
    low = (lane % HEAD_DIM) < (HEAD_DIM // 2)
    swapped = jnp.where(low, pltpu.roll(x, w - HEAD_DIM // 2, 1), pltpu.roll(x, HEAD_DIM // 2, 1))
    return x * cos + swapped * sin_signed


def _rope_kernel(q_ref, k_ref, v_ref, qi_ref, kw_ref, cos_ref, sin_ref, qo_ref, ko_ref, vo_ref, qio_ref, kio_ref):
    cos = cos_ref[...]
    sin = sin_ref[...]
    qo_ref[...] = _rope(q_ref[...], cos, sin).astype(BF16)
    ko_ref[...] = _rope(k_ref[...], cos, sin).astype(BF16)
    vo_ref[0] = v_ref[...].T.astype(BF16)
    cos2 = jnp.concatenate([cos, cos], axis=1)
    sin2 = jnp.concatenate([sin, sin], axis=1)
    qio_ref[...] = _rope(qi_ref[...], cos2, sin2).astype(BF16)
    kr = _rope(kw_ref[...], cos[:, :128], sin[:, :128])
    lane = lax.broadcasted_iota(jnp.int32, kr.shape, 1)
    kio_ref[...] = jnp.where(lane < IDX_DIM, kr, pltpu.roll(kr, IDX_DIM, 1)).astype(BF16)


def _rope_prep(p, cos, sin, b, s):
    t = b * s
    nt = s // TM_R
    c = COL_C // WIDTH
    row = lambda w, col: pl.BlockSpec((TM_R, w), lambda i: (i, col))
    tab = pl.BlockSpec((TM_R, WIDTH), lambda i: (i % nt, 0))
    return pl.pallas_call(
        _rope_kernel,
        grid=(t // TM_R,),
        in_specs=[row(WIDTH, c), row(WIDTH, c + 1), row(WIDTH, c + 2),
                  row(512, COL_CQI // 512), row(128, COL_CKW // 128), tab, tab],
        out_specs=[row(WIDTH, 0), row(WIDTH, 0), pl.BlockSpec((1, WIDTH, TM_R), lambda i: (i, 0, 0)),
                   row(512, 0), row(128, 0)],
        out_shape=[jax.ShapeDtypeStruct((t, WIDTH), BF16)] * 2
        + [jax.ShapeDtypeStruct((t // TM_R, WIDTH, TM_R), BF16),
           jax.ShapeDtypeStruct((t, 512), BF16), jax.ShapeDtypeStruct((t, 128), BF16)],
        compiler_params=_cparams(("parallel",)),
        name="rope_prep",
    )(p, p, p, p, p, cos, sin)


def _rope_tables(s):
    half = HEAD_DIM // 2
    freqs = ROPE_THETA ** (-jnp.arange(half, dtype=F32) / half)
    ang = jnp.arange(s, dtype=jnp.int32).astype(F32)[:, None] * freqs[None, :]
    cos = jnp.cos(ang)
    sin = jnp.sin(ang)
    cos = jnp.tile(jnp.concatenate([cos, cos], axis=1), (1, HEADS))
    sin = jnp.tile(jnp.concatenate([-sin, sin], axis=1), (1, HEADS))
    return cos, sin


def _dsa_kernel(q_ref, k_ref, vt_ref, qi_ref, ki_ref, w_ref, o_ref, hi_ref, lo_ref, *, topk, idx_bits):
    i = pl.program_id(1)
    nkb = (i * TQ_C) // KB_C + 1
    lane128 = lax.broadcasted_iota(jnp.int32, (1, 128), 1)
    lane256 = lax.broadcasted_iota(jnp.int32, (1, WIDTH), 1)
    krow = lax.broadcasted_iota(jnp.int32, (KB_C, 1), 0)
    qchunk = (i * TQ_C + lax.broadcasted_iota(jnp.int32, (1, TQ_C), 1)) // CHUNK
    wt = (w_ref[...] * (IDX_HEADS ** -0.5)).T
    qi = qi_ref[...] * jnp.asarray(IDX_DIM ** -0.5, BF16)
    qi_heads = [jnp.where((lane128 // IDX_DIM) == (h % 2), qi[:, 128 * (h // 2):128 * (h // 2 + 1)],
                          jnp.zeros((), BF16)) for h in range(IDX_HEADS)]

    def score_block(kb, carry):
        off = pl.multiple_of(kb * KB_C, KB_C)
        kib = ki_ref[pl.ds(off, KB_C), :]
        acc = jnp.zeros((KB_C, TQ_C), F32)
        for h in range(IDX_HEADS):
            acc = acc + jnp.maximum(_dot_nt(kib, qi_heads[h]), 0.0) * wt[IDX_DIM + h:IDX_DIM + h + 1, :]
        bits = lax.bitcast_convert_type(acc, jnp.int32)
        key = bits ^ ((bits >> 31) & jnp.int32(0x7FFFFFFF))
        key = jnp.where(acc == 0.0, 0, key)
        adm = ((off + krow) // CHUNK) <= qchunk
        key = jnp.where(adm, key, INT_MIN)
        hi_ref[kb] = (key >> 16).astype(I16)
        lo_ref[kb] = ((key & 0xFFFF) - 32768).astype(I16)
        return carry

    lax.fori_loop(0, nkb, score_block, 0)

    one16 = jnp.ones((), I16)
    zero16 = jnp.zeros((), I16)

    def fold(c):
        c3 = c.reshape(KB_C // 16, 16, TQ_C)
        return functools.reduce(lambda a, b: a + b, [c3[g] for g in range(KB_C // 16)])

    def total(acc):
        return jnp.sum(acc.astype(F32), axis=0, keepdims=True)

    def count(pred):
        def body(kb, acc):
            return acc + fold(jnp.where(pred(kb), one16, zero16))
        return total(lax.fori_loop(0, nkb, body, jnp.zeros((16, TQ_C), I16)))

    def descend(ref, kth):
        def bit(bi, t):
            cand = t + lax.shift_left(jnp.int32(1), 15 - bi)
            c16 = cand.astype(I16)
            return jnp.where(count(lambda kb: ref[kb] >= c16) >= kth, cand, t)
        return lax.fori_loop(0, 16, bit, jnp.full((1, TQ_C), I16_MIN, jnp.int32))

    tau_hi32 = descend(hi_ref, topk)
    tau_hi = tau_hi32.astype(I16)
    kth_lo = topk - count(lambda kb: hi_ref[kb] > tau_hi)

    def mask_low(kb, carry):
        lo_ref[kb] = jnp.where(hi_ref[kb] == tau_hi, lo_ref[kb], jnp.asarray(I16_MIN, I16))
        return carry

    lax.fori_loop(0, nkb, mask_low, 0)
    tau_lo = descend(lo_ref, kth_lo).astype(I16)
    need = kth_lo - count(lambda kb: lo_ref[kb] > tau_lo)
    real = tau_hi32 > I16_MIN
    real16 = tau_hi > jnp.asarray(I16_MIN, I16)

    def is_tie(kb):
        return (hi_ref[kb] == tau_hi) & (lo_ref[kb] == tau_lo)

    ties = count(is_tie)

    def index_search():
        def index_bit(bi, jmax):
            cand = jmax + lax.shift_left(jnp.int32(1), idx_bits - 1 - bi)
            c16 = cand.astype(I16)
            cnt = count(lambda kb: is_tie(kb) & ((kb * KB_C + krow).astype(I16) < c16))
            return jnp.where(cnt < need, cand, jmax)
        return lax.fori_loop(0, idx_bits, index_bit, jnp.zeros((1, TQ_C), jnp.int32))

    excess = jnp.max(jnp.where((ties > need) & real, 1.0, 0.0)) > 0.5
    jmax = lax.cond(excess, index_search, lambda: jnp.full((1, TQ_C), 2 ** (idx_bits - 1), jnp.int32))
    jmax = jmax.astype(I16)

    q = q_ref[...] * jnp.asarray(HEAD_DIM ** -0.5, BF16)
    q4 = jnp.concatenate([jnp.where((lane256 // HEAD_DIM) == h, q, jnp.zeros((), BF16)) for h in range(HEADS)],
                         axis=0)
    half = HEADS * TQ_C // 2

    def attend_block(kb, carry):
        m_old, l_old, acc_old = carry
        off = pl.multiple_of(kb * KB_C, KB_C)
        kblk = k_ref[pl.ds(off, KB_C), :]
        vtb = vt_ref[kb]
        hi = hi_ref[kb]
        lo = lo_ref[kb]
        in_order = (off + krow).astype(I16) <= jmax
        sel = (hi > tau_hi) | ((hi == tau_hi) & real16 & ((lo > tau_lo) | ((lo == tau_lo) & in_order)))
        madd = (jnp.where(sel, one16, zero16).astype(F32) - 1.0) * (-NEG)
        s = jnp.concatenate([_dot_nt(kblk, q4[:half]), _dot_nt(kblk, q4[half:])], axis=1)
        s = s + jnp.concatenate([madd] * HEADS, axis=1)
        m_new = jnp.maximum(m_old, jnp.max(s, axis=0, keepdims=True))
        p = jnp.exp(s - m_new)
        corr = jnp.exp(m_old - m_new)
        l_new = corr * l_old + jnp.sum(p, axis=0, keepdims=True)
        pb = p.astype(BF16)
        pv = jnp.concatenate([_dot(vtb, pb[:, :half]), _dot(vtb, pb[:, half:])], axis=1)
        return m_new, l_new, corr * acc_old + pv

    cols = HEADS * TQ_C
    init = (jnp.full((1, cols), NEG, F32), jnp.zeros((1, cols), F32), jnp.zeros((WIDTH, cols), F32))
    _, l_fin, acc_fin = lax.fori_loop(0, nkb, attend_block, init)
    ot = acc_fin / l_fin
    out = jnp.zeros((TQ_C, WIDTH), F32)
    for h in range(HEADS):
        out = jnp.where((lane256 // HEAD_DIM) == h, ot[:, h * TQ_C:(h + 1) * TQ_C].T, out)
    o_ref[...] = out


def _dsa(p, qr, kr, vt, qir, kir, b, s):
    nt = s // TQ_C
    nkb = s // KB_C
    topk = min(TOPK_MAX, s // 4)
    idx_bits = int(math.log2(s)) + 1
    qrow = lambda w, col: pl.BlockSpec((TQ_C, w), lambda bb, i: (bb * nt + i, col))
    full = lambda w: pl.BlockSpec((s, w), lambda bb, i: (bb, 0))
    return pl.pallas_call(
        functools.partial(_dsa_kernel, topk=topk, idx_bits=idx_bits),
        grid=(b, nt),
        in_specs=[qrow(WIDTH, 0), full(WIDTH), pl.BlockSpec((nkb, WIDTH, KB_C), lambda bb, i: (bb, 0, 0)),
                  qrow(512, 0), full(128), qrow(128, COL_CKW // 128)],
        out_specs=qrow(WIDTH, 0),
        out_shape=jax.ShapeDtypeStruct((b * s, WIDTH), F32),
        scratch_shapes=[
            pltpu.VMEM((nkb, KB_C, TQ_C), I16),
            pltpu.VMEM((nkb, KB_C, TQ_C), I16),
        ],
        compiler_params=_cparams(("parallel", "arbitrary")),
        name="dsa",
    )(qr, kr, vt, qir, kir, p)


TM_D = 256
TS_D = 256
assert TS_D == WIDTH and CHUNK == HEAD_DIM


def _dprep_kernel(r_ref, k_ref, v_ref, wa_ref, rp_ref, kp_ref, vp_ref, wap_ref,
                  mur_ref, muk_ref, muv_ref, muwa_ref, w0_ref, w2_ref, a0_ref, kkk_ref, ka_ref,
                  ro_ref, lwo_ref, kto_ref, vo_ref, kko_ref, ao_ref, *, seq):
    first = (pl.program_id(0) * TM_D) % seq == 0

    def shift(x_ref, p_ref, mu_ref):
        x = x_ref[...]
        row0 = jnp.where(first, 0.0, p_ref[7:8, :])
        rid = lax.broadcasted_iota(jnp.int32, x.shape, 0)
        prev = jnp.where(rid == 0, row0, pltpu.roll(x, 1, 0))
        return x + mu_ref[...] * (prev - x)

    r = shift(r_ref, rp_ref, mur_ref)
    k = shift(k_ref, kp_ref, muk_ref)
    v = shift(v_ref, vp_ref, muv_ref)
    wa = shift(wa_ref, wap_ref, muwa_ref)
    lane = lax.broadcasted_iota(jnp.int32, wa.shape, 1)
    lora_in = jnp.where(lane < LORA, jnp.tanh(wa), wa).astype(BF16)
    lora = _dot(lora_in, w2_ref[...])
    z = w0_ref[...] + lora[:, :WIDTH]
    wlog = -(jnp.maximum(-z, 0.0) + jnp.log1p(jnp.exp(-jnp.abs(z)))) - 0.5
    a = _sigmoid(a0_ref[...] + lora[:, WIDTH:])
    ro_ref[...] = r
    lwo_ref[...] = -jnp.exp(wlog)
    kto_ref[...] = k * (1.0 + (a - 1.0) * ka_ref[...])
    vo_ref[...] = v
    kko_ref[...] = k * kkk_ref[...]
    ao_ref[...] = a


def _dprep(p, mu, w0, w2, a0, kkk, ka, b, s):
    t = b * s
    rb = TM_D // 8
    c = COL_D // WIDTH
    row = lambda w, col: pl.BlockSpec((TM_D, w), lambda i: (i, col))
    prev = lambda w, col: pl.BlockSpec((8, w), lambda i: (jnp.maximum(i * rb - 1, 0), col))
    vec = lambda w: pl.BlockSpec((1, w), lambda i: (0, 0))
    mur, muk, muv = mu[:, 0:256], mu[:, 256:512], mu[:, 512:768]
    muwa = mu[:, 768:896]
    return pl.pallas_call(
        functools.partial(_dprep_kernel, seq=s),
        grid=(t // TM_D,),
        in_specs=[row(WIDTH, c), row(WIDTH, c + 1), row(WIDTH, c + 2), row(128, COL_DWA // 128),
                  prev(WIDTH, c), prev(WIDTH, c + 1), prev(WIDTH, c + 2), prev(128, COL_DWA // 128),
                  vec(WIDTH), vec(WIDTH), vec(WIDTH), vec(128), vec(WIDTH),
                  pl.BlockSpec((128, 2 * WIDTH), lambda i: (0, 0)), vec(WIDTH), vec(WIDTH), vec(WIDTH)],
        out_specs=[row(WIDTH, 0)] * 6,
        out_shape=[jax.ShapeDtypeStruct((t, WIDTH), F32)] * 6,
        compiler_params=_cparams(("parallel",)),
        name="rwkv_prep",
    )(p, p, p, p, p, p, p, p, mur, muk, muv, muwa, w0, w2, a0, kkk, ka)


def _rwkv_kernel(r_ref, lw_ref, kt_ref, v_ref, kk_ref, a_ref, rk_ref, gg_ref, gb_ref, o_ref, st_ref):
    @pl.when(pl.program_id(1) == 0)
    def _():
        st_ref[...] = jnp.zeros(st_ref.shape, F32)

    tt = lax.broadcasted_iota(jnp.int32, (TS_D, TS_D), 0)
    ss = lax.broadcasted_iota(jnp.int32, (TS_D, TS_D), 1)
    same = (tt // CHUNK) == (ss // CHUNK)
    strict = same & (tt > ss)
    incl = same & (tt >= ss)
    eye = jnp.where(tt == ss, 1.0, 0.0)
    ones_bd = jnp.where(same, 1.0, 0.0).astype(BF16)
    ltri_bd = jnp.where(incl, 1.0, 0.0).astype(BF16)
    hid = lax.broadcasted_iota(jnp.int32, (1, WIDTH), 1) // HEAD_DIM

    def group_sum(x):
        return _dot3(x, ones_bd, left=True)

    rr, lw, kt, vv, kkr, aa = (x[...] for x in (r_ref, lw_ref, kt_ref, v_ref, kk_ref, a_ref))
    kk = kkr / jnp.maximum(jnp.sqrt(group_sum(kkr * kkr)), 1e-12)
    be = kk * aa
    b = _dot3(ltri_bd, lw, left=False)
    bl = _dot3(ones_bd, lw, left=False)
    eb = jnp.exp(b)
    enb = jnp.exp(-b)
    dec = jnp.exp(bl - b)
    qa = -kk * jnp.exp(b - lw)
    qr = rr * eb
    q2 = jnp.concatenate([qa, qr], axis=0)
    k2 = jnp.concatenate([be * enb, kt * enb], axis=0).astype(BF16)
    zero = jnp.zeros((TS_D, WIDTH), F32)
    sa0, qt, qo, o0 = zero, zero, zero, zero
    for h in range(HEADS):
        hm = hid == h
        g = _dot_nt(jnp.where(hm, q2, 0.0).astype(BF16), k2)
        a_ab = jnp.where(strict, g[:TS_D, :TS_D], 0.0)
        a_ak = jnp.where(strict, g[:TS_D, TS_D:], 0.0).astype(BF16)
        a_rb = jnp.where(incl, g[TS_D:, :TS_D], 0.0).astype(BF16)
        a_rk = jnp.where(incl, g[TS_D:, TS_D:], 0.0).astype(BF16)
        tm = eye + a_ab
        pw = a_ab.astype(BF16)
        for _ in range(int(math.log2(CHUNK)) - 1):
            pwf = _dot(pw, pw)
            pw = pwf.astype(BF16)
            tm = tm + _dot(tm.astype(BF16), pw)
        tmb = tm.astype(BF16)
        vh = jnp.where(hm, vv, 0.0).astype(BF16)
        sa0_h = _dot(tmb, _dot(a_ak, vh).astype(BF16))
        qt_h = _dot(tmb, jnp.where(hm, qa, 0.0).astype(BF16))
        sa0 = sa0 + sa0_h
        qt = qt + qt_h
        qo = qo + jnp.where(hm, qr, 0.0) + _dot(a_rb, qt_h.astype(BF16))
        o0 = o0 + _dot(a_rb, sa0_h.astype(BF16)) + _dot(a_rk, vh)
    bd = be * dec
    kd = kt * dec
    outs = []
    for c in range(TS_D // CHUNK):
        rows = slice(c * CHUNK, (c + 1) * CHUNK)
        st = st_ref[...]
        stb = st.astype(BF16)
        sa_c = sa0[rows] + _dot_nt(qt[rows].astype(BF16), stb)
        outs.append(o0[rows] + _dot_nt(qo[rows].astype(BF16), stb))
        lhs_t = jnp.concatenate([sa_c, vv[rows]], axis=0).T.astype(BF16)
        rhs = jnp.concatenate([bd[rows], kd[rows]], axis=0).astype(BF16)
        upd = _dot(lhs_t, rhs)
        st_ref[...] = st * jnp.exp(bl[c * CHUNK:c * CHUNK + 1, :]) + jnp.where(same, upd, 0.0)
    o = jnp.concatenate(outs, axis=0)
    mean = group_sum(o) * (1.0 / HEAD_DIM)
    cen = o - mean
    var = group_sum(cen * cen) * (1.0 / HEAD_DIM)
    o = cen * lax.rsqrt(var + GN_EPS) * gg_ref[...] + gb_ref[...]
    o_ref[...] = o + group_sum(rr * kt * rk_ref[...]) * vv


def _rwkv7(r, lw, kt, v, kk, a, rk, gg, gb, b, s):
    nt = s // TS_D
    row = pl.BlockSpec((TS_D, WIDTH), lambda bb, i: (bb * nt + i, 0))
    vec = pl.BlockSpec((1, WIDTH), lambda bb, i: (0, 0))
    return pl.pallas_call(
        _rwkv_kernel,
        grid=(b, nt),
        in_specs=[row] * 6 + [vec] * 3,
        out_specs=row,
        out_shape=jax.ShapeDtypeStruct((b * s, WIDTH), F32),
        scratch_shapes=[pltpu.VMEM((WIDTH, WIDTH), F32)],
        compiler_params=_cparams(("parallel", "arbitrary")),
        name="rwkv7",
    )(r, lw, kt, v, kk, a, rk, gg, gb)


TM_M = 256


def _merge_kernel(oa_ref, ob_ref, oc_ref, od_ref, za_ref, zb_ref, zc_ref, zd_ref,
                  ga_ref, gb_ref, gc_ref, gd_ref, x_ref, wb_ref, wo_ref, fg_ref, o_ref, *, final):
    def branch(o_r, z_r, g_r, idx):
        z = z_r[...]
        u = (o_r[...] * (z * _sigmoid(z))).astype(BF16)
        return _sigmoid(g_r[...]) * _dot(u, wb_ref[idx])

    y = (branch(oa_ref, za_ref, ga_ref, 0) + branch(ob_ref, zb_ref, gb_ref, 1)
         + branch(oc_ref, zc_ref, gc_ref, 2) + branch(od_ref, zd_ref, gd_ref, 3))
    xn = x_ref[...] + _dot(y.astype(BF16), wo_ref[...])
    if final:
        inv = lax.rsqrt(jnp.mean(xn * xn, axis=-1, keepdims=True) + EPS)
        xn = (xn * inv) * fg_ref[...]
    o_ref[...] = xn


def _merge(oa, ob, oc, od, p, x2, wb, wo, fg, final):
    t = x2.shape[0]
    row = lambda w, col: pl.BlockSpec((TM_M, w), lambda i: (i, col))
    zcol = lambda base: (base + 3 * WIDTH) // WIDTH
    gcol = lambda n: COL_G // D_MODEL + n
    return pl.pallas_call(
        functools.partial(_merge_kernel, final=final),
        grid=(t // TM_M,),
        in_specs=[row(WIDTH, 0)] * 4
        + [row(WIDTH, zcol(COL_A)), row(WIDTH, zcol(COL_B)), row(WIDTH, zcol(COL_C)), row(WIDTH, COL_DZ // WIDTH)]
        + [row(D_MODEL, gcol(n)) for n in range(4)]
        + [row(D_MODEL, 0),
           pl.BlockSpec((4, WIDTH, D_MODEL), lambda i: (0, 0, 0)),
           pl.BlockSpec((D_MODEL, D_MODEL), lambda i: (0, 0)),
           pl.BlockSpec((1, D_MODEL), lambda i: (0, 0))],
        out_specs=row(D_MODEL, 0),
        out_shape=jax.ShapeDtypeStruct((t, D_MODEL), F32),
        compiler_params=_cparams(("parallel",)),
        name="merge",
    )(oa, ob, oc, od, p, p, p, p, p, p, p, p, x2, wb, wo, fg)


def _pad_w_in(w):
    def seg(lo, hi, width):
        s = w[:, lo:hi]
        return jnp.pad(s, ((0, 0), (0, width - (hi - lo))))
    c_end = 2048 + 1608
    d_end = c_end + 896
    parts = [w[:, :2048], seg(2048, c_end, COL_D - COL_C), seg(c_end, d_end, COL_DZ - COL_D), w[:, d_end:]]
    return jnp.concatenate(parts, axis=1).astype(BF16)


def kernel(x, norm_g, w_in, a_rel_bias, b_lb_logits, b_norm_g, d_mu, d_w0, d_w_up, d_a0, d_a_up,
           d_k_k, d_k_a, d_r_k, d_gn_g, d_gn_b, w_branch, w_out, final_g):
    b, s, d = x.shape
    depth = norm_g.shape[0]
    t = b * s
    x2 = x.reshape(t, d)
    cos, sin = _rope_tables(s)
    lb_all = jnp.cumsum(jax.nn.softmax(b_lb_logits.astype(F32), axis=0), axis=0)
    lb_all = lb_all - lb_all[0:1]
    tm_in = next(m for m in (2048, 1024, 512, 256) if t % m == 0)
    for l in range(depth):
        p = _inproj(x2, norm_g[l][None, :], _pad_w_in(w_in[l]), tm_in, 512)
        oa = _band_attention(p, _band_bias(a_rel_bias[l]), b, s)
        lb = lb_all[l][None, :]
        ob = _hgrn2(p, jnp.log(lb), jnp.log1p(-lb), 1.0 - lb, b_norm_g[l][None, :], b, s)
        qr, kr, vt, qir, kir = _rope_prep(p, cos, sin, b, s)
        oc = _dsa(p, qr, kr, vt, qir, kir, b, s)
        zeros = jnp.zeros((LORA, WIDTH), F32)
        w2 = jnp.concatenate([jnp.concatenate([d_w_up[l], zeros], axis=1),
                              jnp.concatenate([zeros, d_a_up[l]], axis=1)], axis=0).astype(BF16)
        r, lw, kt, v, kk, a = _dprep(p, d_mu[l][None, :], d_w0[l][None, :], w2, d_a0[l][None, :],
                                     d_k_k[l][None, :], d_k_a[l][None, :], b, s)
        od = _rwkv7(r, lw, kt, v, kk, a, d_r_k[l].reshape(1, WIDTH), d_gn_g[l][None, :], d_gn_b[l][None, :], b, s)
        x2 = _merge(oa, ob, oc, od, p, x2, w_branch[l].astype(BF16), w_out[l].astype(BF16),
                    final_g[None, :], l == depth - 1)
    return x2.reshape(b, s, d)
```

```python
import functools
import itertools
import math

import numpy as np
import jax
import jax.numpy as jnp
from jax import lax
from jax.experimental import pallas as pl
from jax.experimental.pallas import tpu as pltpu

F32 = jnp.float32
BF16 = jnp.bfloat16

D_MODEL = 1024
HEADS = 4
HEAD_DIM = 64
WIDTH = HEADS * HEAD_DIM
CHUNK = 64
A_LEFT_CHUNKS = 8
REL_CLIP = 128
IDX_HEADS = 8
IDX_DIM = 64
TOPK_MAX = 256
LORA = 64
ROPE_THETA = 10000.0
EPS = 1e-6
GN_EPS = 64e-5
NEG = -1e30
INT_MIN = -2 ** 31
I16 = jnp.int16
I16_MIN = -2 ** 15
COARSE = jnp.bfloat16
KEY_NEG_INF = INT_MIN + 0x7FFFFF

H_CQI = 0
H_CKW = 512
H_BF = 768
H_D = 1024
H_DWA = 1792
NPH = 2048
L_A = 0
L_BQ = 1024
L_C = 1792
L_DZ = 2816
L_G = 3072
NPL = 7168
VMEM_LIMIT = 56 * 1024 * 1024

_NT = (((1,), (1,)), ((), ()))


def _sigmoid(x):
    return 1.0 / (1.0 + jnp.exp(-x))


def _dot(a, b, precision=None):
    return jnp.dot(a, b, preferred_element_type=F32, precision=precision)


def _dot_nt(a, b, precision=None):
    return lax.dot_general(a, b, _NT, preferred_element_type=F32, precision=precision)


def _dot3(a, b, left):
    x = a if left else b
    hi = x.astype(BF16)
    rest = x - hi.astype(F32)
    mid = rest.astype(BF16)
    lo = (rest - mid.astype(F32)).astype(BF16)
    if left:
        return _dot(hi, b) + _dot(mid, b) + _dot(lo, b)
    return _dot(a, hi) + _dot(a, mid) + _dot(a, lo)


def _cparams(sem):
    return pltpu.CompilerParams(dimension_semantics=sem, vmem_limit_bytes=VMEM_LIMIT)


def _inproj_kernel(x_ref, g_ref, w_ref, o_ref, h_ref):
    @pl.when(pl.program_id(1) == 0)
    def _():
        x = x_ref[...]
        inv = lax.rsqrt(jnp.mean(x * x, axis=-1, keepdims=True) + EPS)
        h_ref[...] = ((x * inv) * g_ref[...]).astype(BF16)

    o_ref[...] = _dot(h_ref[...], w_ref[...]).astype(o_ref.dtype)


def _inproj(x2, g, w, tm, tn, dtype):
    t = x2.shape[0]
    n = w.shape[1]
    return pl.pallas_call(
        _inproj_kernel,
        grid=(t // tm, n // tn),
        in_specs=[
            pl.BlockSpec((tm, D_MODEL), lambda i, j: (i, 0)),
            pl.BlockSpec((1, D_MODEL), lambda i, j: (0, 0)),
            pl.BlockSpec((D_MODEL, tn), lambda i, j: (0, j)),
        ],
        out_specs=pl.BlockSpec((tm, tn), lambda i, j: (i, j)),
        out_shape=jax.ShapeDtypeStruct((t, n), dtype),
        scratch_shapes=[pltpu.VMEM((tm, D_MODEL), BF16)],
        compiler_params=_cparams(("parallel", "arbitrary")),
        name="inproj",
    )(x2, g, w)


TQ_A = 256
WIN_A = 3 * TQ_A


def _band_kernel(q_ref, k0_ref, k1_ref, k2_ref, v0_ref, v1_ref, v2_ref, bias_ref, o_ref):
    i = pl.program_id(1)
    q = q_ref[...]
    kc = jnp.concatenate([k0_ref[...], k1_ref[...], k2_ref[...]], axis=0).astype(BF16)
    vc = jnp.concatenate([v0_ref[...], v1_ref[...], v2_ref[...]], axis=0).astype(BF16)
    lane = lax.broadcasted_iota(jnp.int32, (1, WIDTH), 1)
    kpos = lax.broadcasted_iota(jnp.int32, (1, WIN_A), 1) + (i - 2) * TQ_A
    kvalid = kpos >= 0
    out = jnp.zeros((TQ_A, WIDTH), F32)
    for h in range(HEADS):
        hm = (lane // HEAD_DIM) == h
        qh = jnp.where(hm, q, jnp.zeros((), q.dtype)).astype(BF16)
        s = _dot_nt(qh, kc) * (HEAD_DIM ** -0.5) + bias_ref[h]
        s = jnp.where(kvalid, s, NEG)
        m = jnp.max(s, axis=-1, keepdims=True)
        p = jnp.exp(s - m)
        l = jnp.sum(p, axis=-1, keepdims=True)
        o = _dot(p.astype(BF16), vc) / l
        out = jnp.where(hm, o, out)
    o_ref[...] = out


def _band_attention(p, bias, b, s):
    nt = s // TQ_A
    qcol, kcol, vcol = L_A // WIDTH, L_A // WIDTH + 1, L_A // WIDTH + 2

    def kv_spec(col, back):
        return pl.BlockSpec((TQ_A, WIDTH), lambda bb, i: (bb * nt + jnp.maximum(i - back, 0), col))

    return pl.pallas_call(
        _band_kernel,
        grid=(b, nt),
        in_specs=[
            pl.BlockSpec((TQ_A, WIDTH), lambda bb, i: (bb * nt + i, qcol)),
            kv_spec(kcol, 2), kv_spec(kcol, 1), kv_spec(kcol, 0),
            kv_spec(vcol, 2), kv_spec(vcol, 1), kv_spec(vcol, 0),
            pl.BlockSpec((HEADS, TQ_A, WIN_A), lambda bb, i: (0, 0, 0)),
        ],
        out_specs=pl.BlockSpec((TQ_A, WIDTH), lambda bb, i: (bb * nt + i, 0)),
        out_shape=jax.ShapeDtypeStruct((b * s, WIDTH), F32),
        compiler_params=_cparams(("parallel", "parallel")),
        name="band_attention",
    )(p, p, p, p, p, p, p, bias)


def _band_bias(rel_bias):
    tab = rel_bias.astype(F32)
    n = tab.shape[1]
    d_lo, d_hi = 2 * TQ_A - (WIN_A - 1), 2 * TQ_A + TQ_A - 1
    n_lo = -(CHUNK - 1) - d_lo + 1
    n_hi = d_hi - REL_CLIP + 1
    v = jnp.concatenate([jnp.repeat(tab[:, :1], n_lo, axis=1), tab[:, 1:n - 1],
                         jnp.repeat(tab[:, n - 1:], n_hi, axis=1)], axis=1)
    span = WIN_A + TQ_A
    u = jnp.concatenate([v[:, :WIN_A][:, ::-1], jnp.zeros((HEADS, span - v.shape[1]), F32),
                         v[:, WIN_A:][:, ::-1]], axis=1)
    skew = jnp.tile(u, (1, TQ_A))[:, :TQ_A * (span - 1)].reshape(HEADS, TQ_A, span - 1)[:, :, :WIN_A]
    ql = np.arange(TQ_A)[:, None]
    kl = np.arange(WIN_A)[None, :]
    lo = CHUNK * (ql // CHUNK)
    band = (kl >= lo) & (kl < lo + (A_LEFT_CHUNKS + 1) * CHUNK)
    return jnp.where(band[None], skew, NEG)


TS_B = 256
NB_B = 4
_LEVELS = (1, 2, 4, 8, 16, 32)
assert TS_B == WIDTH and CHUNK == HEAD_DIM


def _hgrn_kernel(q_ref, f_ref, i_ref, la_ref, l1_ref, oml_ref, ng_ref, o_ref, st_ref):
    @pl.when(pl.program_id(1) == 0)
    def _():
        st_ref[...] = jnp.zeros(st_ref.shape, F32)

    steps = [_hgrn_step(q_ref.at[bi], f_ref.at[bi], i_ref.at[bi], la_ref, l1_ref, oml_ref, ng_ref,
                        o_ref.at[bi], st_ref.at[bi]) for bi in range(q_ref.shape[0])]
    for _ in itertools.zip_longest(*steps):
        pass


def _hgrn_step(q_ref, f_ref, i_ref, la_ref, l1_ref, oml_ref, ng_ref, o_ref, st_ref):
    fl = f_ref[...]
    logsig = jnp.minimum(fl, 0.0) - jnp.log1p(jnp.exp(-jnp.abs(fl)))
    y = l1_ref[...] + logsig
    la = la_ref[...]
    g = jnp.maximum(la, y) + jnp.log1p(jnp.exp(-jnp.abs(la - y)))
    kk = oml_ref[...] * _sigmoid(-fl)
    qq = q_ref[...].astype(F32)
    qf = qq * _sigmoid(qq)
    vv = i_ref[...].astype(F32)
    tt = lax.broadcasted_iota(jnp.int32, (TS_B, TS_B), 0)
    ss = lax.broadcasted_iota(jnp.int32, (TS_B, TS_B), 1)
    same = (tt // CHUNK) == (ss // CHUNK)
    ones_bd = jnp.where(same, 1.0, 0.0).astype(BF16)
    ltri_bd = jnp.where(same & (tt >= ss), 1.0, 0.0).astype(BF16)
    hid = lax.broadcasted_iota(jnp.int32, (1, WIDTH), 1) // HEAD_DIM
    heads4 = lambda x: jnp.concatenate([jnp.where(hid == h, x, 0.0) for h in range(HEADS)], axis=0).astype(BF16)
    tile4 = lambda mask: jnp.concatenate([mask] * HEADS, axis=0)

    b = _dot3(ltri_bd, g, left=False)
    bl = _dot3(ones_bd, g, left=False)
    att4 = jnp.where(tile4(tt == ss), _dot_nt(heads4(qf), kk.astype(BF16)), 0.0)
    for m in _LEVELS:
        pick = jnp.where(ss == (tt // (2 * m)) * (2 * m) + (m - 1), 1.0, 0.0).astype(BF16)
        b_hi = b.astype(BF16)
        br = _dot(pick, b_hi) + _dot(pick, (b - b_hi.astype(F32)).astype(BF16))
        ql = qf * jnp.exp(jnp.minimum(b - br, 1.0))
        kl = kk * jnp.exp(jnp.minimum(br - b, 1.0))
        mask = ((tt // (2 * m)) == (ss // (2 * m))) & ((tt % (2 * m)) >= m) & ((ss % (2 * m)) < m)
        att4 = att4 + jnp.where(tile4(mask), _dot_nt(heads4(ql), kl.astype(BF16)), 0.0)
        yield
    att4 = att4.astype(BF16)
    o = jnp.zeros((TS_B, WIDTH), F32)
    for h in range(HEADS):
        o = o + _dot(att4[h * TS_B:(h + 1) * TS_B], jnp.where(hid == h, vv, 0.0).astype(BF16))
        yield
    qe = (qf * jnp.exp(b)).astype(BF16)
    kd = (kk * jnp.exp(bl - b)).astype(BF16)
    outs = []
    for c in range(TS_B // CHUNK):
        rows = slice(c * CHUNK, (c + 1) * CHUNK)
        st = st_ref[...]
        outs.append(_dot_nt(qe[rows], st.astype(BF16)))
        upd = _dot(vv[rows].T.astype(BF16), kd[rows])
        st_ref[...] = st * jnp.exp(bl[c * CHUNK:c * CHUNK + 1, :]) + jnp.where(same, upd, 0.0)
        yield
    o = o + jnp.concatenate(outs, axis=0)
    ms = _dot3(o * o, ones_bd, left=True) * (1.0 / HEAD_DIM)
    o_ref[...] = o * lax.rsqrt(ms + EPS) * ng_ref[...]


def _hgrn2(ph, pl16, la, l1, oml, ng, b, s):
    nt = s // TS_B
    nb = NB_B if b % NB_B == 0 else 1
    row = lambda col: pl.BlockSpec((nb, TS_B, WIDTH), lambda bb, i: (bb, i, col))
    vec = pl.BlockSpec((1, WIDTH), lambda bb, i: (0, 0))
    ph3 = ph.reshape(b, s, NPH)
    pl3 = pl16.reshape(b, s, NPL)
    out = pl.pallas_call(
        _hgrn_kernel,
        grid=(b // nb, nt),
        in_specs=[row(L_BQ // WIDTH), row(H_BF // WIDTH), row(L_BQ // WIDTH + 1), vec, vec, vec, vec],
        out_specs=row(0),
        out_shape=jax.ShapeDtypeStruct((b, s, WIDTH), F32),
        scratch_shapes=[pltpu.VMEM((nb, WIDTH, WIDTH), F32)],
        compiler_params=_cparams(("parallel", "arbitrary")),
        name="hgrn2",
    )(pl3, ph3, pl3, la, l1, oml, jnp.tile(ng, (1, HEADS)))
    return out.reshape(b * s, WIDTH)


TQ_C = 512
KB_C = 512
TM_R = KB_C


def _rope(x, cos, sin_signed):
    w = x.shape[-1]
    lane = lax.broadcasted_iota(jnp.int32, x.shape, 1)
    low = (lane % HEAD_DIM) < (HEAD_DIM // 2)
    swapped = jnp.where(low, pltpu.roll(x, w - HEAD_DIM // 2, 1), pltpu.roll(x, HEAD_DIM // 2, 1))
    return x * cos + swapped * sin_signed


def _rope_kernel(q_ref, k_ref, v_ref, qi_ref, kw_ref, cos_ref, sin_ref, qo_ref, ko_ref, vo_ref, qio_ref, kio_ref):
    cos = cos_ref[...]
    sin = sin_ref[...]
    qo_ref[...] = _rope(q_ref[...].astype(F32), cos, sin).astype(BF16)
    ko_ref[...] = _rope(k_ref[...].astype(F32), cos, sin).astype(BF16)
    vo_ref[0] = v_ref[...].astype(F32).T.astype(BF16)
    cos2 = jnp.concatenate([cos, cos], axis=1)
    sin2 = jnp.concatenate([sin, sin], axis=1)
    qio_ref[...] = _rope(qi_ref[...], cos2, sin2).astype(BF16)
    kr = _rope(kw_ref[...], cos[:, :128], sin[:, :128])
    lane = lax.broadcasted_iota(jnp.int32, kr.shape, 1)
    kio_ref[...] = jnp.where(lane < IDX_DIM, kr, pltpu.roll(kr, IDX_DIM, 1)).astype(BF16)


def _rope_prep(ph, pl16, cos, sin, b, s):
    t = b * s
    nt = s // TM_R
    c = L_C // WIDTH
    row = lambda w, col: pl.BlockSpec((TM_R, w), lambda i: (i, col))
    tab = pl.BlockSpec((TM_R, WIDTH), lambda i: (i % nt, 0))
    return pl.pallas_call(
        _rope_kernel,
        grid=(t // TM_R,),
        in_specs=[row(WIDTH, c), row(WIDTH, c + 1), row(WIDTH, c + 2),
                  row(512, H_CQI // 512), row(128, H_CKW // 128), tab, tab],
        out_specs=[row(WIDTH, 0), row(WIDTH, 0), pl.BlockSpec((1, WIDTH, TM_R), lambda i: (i, 0, 0)),
                   row(512, 0), row(128, 0)],
        out_shape=[jax.ShapeDtypeStruct((t, WIDTH), BF16)] * 2
        + [jax.ShapeDtypeStruct((t // TM_R, WIDTH, TM_R), BF16),
           jax.ShapeDtypeStruct((t, 512), BF16), jax.ShapeDtypeStruct((t, 128), BF16)],
        compiler_params=_cparams(("parallel",)),
        name="rope_prep",
    )(pl16, pl16, pl16, ph, ph, cos, sin)


def _rope_tables(s):
    half = HEAD_DIM // 2
    freqs = ROPE_THETA ** (-jnp.arange(half, dtype=F32) / half)
    ang = jnp.arange(s, dtype=jnp.int32).astype(F32)[:, None] * freqs[None, :]
    cos = jnp.cos(ang)
    sin = jnp.sin(ang)
    cos = jnp.tile(jnp.concatenate([cos, cos], axis=1), (1, HEADS))
    sin = jnp.tile(jnp.concatenate([-sin, sin], axis=1), (1, HEADS))
    return cos, sin


def _dsa_kernel(q_ref, k_ref, vt_ref, qi_ref, ki_ref, w_ref, o_ref, xb_ref, sf_ref, *, topk, idx_bits):
    i = pl.program_id(1)
    nkb = ((i + 1) * TQ_C + KB_C - 1) // KB_C
    lane128 = lax.broadcasted_iota(jnp.int32, (1, 128), 1)
    lane256 = lax.broadcasted_iota(jnp.int32, (1, WIDTH), 1)
    krow = lax.broadcasted_iota(jnp.int32, (KB_C, 1), 0)
    qchunk = (i * TQ_C + lax.broadcasted_iota(jnp.int32, (1, TQ_C), 1)) // CHUNK
    wt = (w_ref[...] * (IDX_HEADS ** -0.5)).T
    qi = qi_ref[...] * jnp.asarray(IDX_DIM ** -0.5, BF16)
    qi_heads = [jnp.where((lane128 // IDX_DIM) == (h % 2), qi[:, 128 * (h // 2):128 * (h // 2 + 1)],
                          jnp.zeros((), BF16)) for h in range(IDX_HEADS)]

    def score_block(kb, carry):
        off = pl.multiple_of(kb * KB_C, KB_C)
        kib = ki_ref[pl.ds(off, KB_C), :]
        acc = jnp.zeros((KB_C, TQ_C), F32)
        for h in range(IDX_HEADS):
            acc = acc + jnp.maximum(_dot_nt(kib, qi_heads[h]), 0.0) * wt[IDX_DIM + h:IDX_DIM + h + 1, :]
        adm = ((off + krow) // CHUNK) <= qchunk
        sc = jnp.where(adm, acc, -jnp.inf)
        sf_ref[kb] = sc
        xb_ref[kb] = sc.astype(COARSE)
        return carry

    lax.fori_loop(0, nkb, score_block, 0)

    def float_of_key(key):
        key = jnp.maximum(key, KEY_NEG_INF)
        return lax.bitcast_convert_type(key ^ ((key >> 31) & jnp.int32(0x7FFFFFFF)), F32)

    def coarse_of_key(key16):
        return float_of_key(jnp.where(key16 < 0, (key16 << 16) + 0xFFFF, key16 << 16))

    one16 = jnp.ones((), I16)
    zero16 = jnp.zeros((), I16)

    def count16(pred):
        def body(kb, acc):
            c3 = jnp.where(pred(xb_ref[kb]), one16, zero16).reshape(KB_C // 16, 16, TQ_C)
            return acc + functools.reduce(lambda a, b: a + b, [c3[g] for g in range(KB_C // 16)])
        acc = lax.fori_loop(0, nkb, body, jnp.zeros((16, TQ_C), I16))
        return jnp.sum(acc.astype(F32), axis=0, keepdims=True)

    def count32(pred):
        def body(kb, acc):
            c3 = jnp.where(pred(kb, sf_ref[kb]), 1.0, 0.0).reshape(KB_C // 8, 8, TQ_C)
            return acc + functools.reduce(lambda a, b: a + b, [c3[g] for g in range(KB_C // 8)])
        acc = lax.fori_loop(0, nkb, body, jnp.zeros((8, TQ_C), F32))
        return jnp.sum(acc, axis=0, keepdims=True)

    def coarse_bit(bi, t):
        cand = t + lax.shift_left(jnp.int32(1), 15 - bi)
        cb = coarse_of_key(cand).astype(COARSE)
        return jnp.where(count16(lambda x: x >= cb) >= topk, cand, t)

    t16 = lax.fori_loop(0, 16, coarse_bit, jnp.full((1, TQ_C), I16_MIN, jnp.int32))
    tau_c = coarse_of_key(t16)
    next_c = coarse_of_key(t16 + 1)
    tau_cb = tau_c.astype(COARSE)
    next_cb = next_c.astype(COARSE)
    real = tau_c > -jnp.inf
    above_cb = jnp.where(real, next_c, float(jnp.finfo(COARSE).min)).astype(COARSE)
    kth_fine = topk - count16(lambda x: x >= next_cb)

    def mask_bucket(kb, carry):
        x = xb_ref[kb]
        inside = jnp.where((x >= tau_cb) & jnp.logical_not(x >= next_cb), one16, zero16).astype(F32)
        sf_ref[kb] = jnp.where(inside > 0.5, sf_ref[kb], -jnp.inf)
        return carry

    lax.fori_loop(0, nkb, mask_bucket, 0)

    t16s = jnp.where(t16 < 0, (t16 << 16) + 0xFFFF, t16 << 16)

    def fine_bit(bi, t):
        cand = t + lax.shift_left(jnp.int32(1), 16 - bi)
        cf = float_of_key(cand)
        return jnp.where(count32(lambda kb, sc: sc >= cf) >= kth_fine, cand, t)

    tau = float_of_key(lax.fori_loop(0, 17, fine_bit, t16s - 0x8000))
    need = kth_fine - count32(lambda kb, sc: sc > tau)
    ties = count32(lambda kb, sc: sc == tau)

    def index_search():
        def index_bit(bi, jmax):
            cand = jmax + lax.shift_left(jnp.int32(1), idx_bits - 1 - bi)
            cnt = count32(lambda kb, sc: (sc == tau) & ((kb * KB_C + krow) < cand))
            return jnp.where(cnt < need, cand, jmax)
        return lax.fori_loop(0, idx_bits, index_bit, jnp.zeros((1, TQ_C), jnp.int32))

    excess = jnp.max(jnp.where((ties > need) & real, 1.0, 0.0)) > 0.5
    jmax = lax.cond(excess, index_search, lambda: jnp.full((1, TQ_C), 2 ** (idx_bits - 1), jnp.int32))

    q = q_ref[...] * jnp.asarray(HEAD_DIM ** -0.5, BF16)
    q4 = jnp.concatenate([jnp.where((lane256 // HEAD_DIM) == h, q, jnp.zeros((), BF16)) for h in range(HEADS)],
                         axis=0)
    half = HEADS * TQ_C // 2

    def attend(blocks, carry):
        m_old, l_old, acc_old = carry
        scores = []
        for kb in blocks:
            off = pl.multiple_of(kb * KB_C, KB_C)
            kblk = k_ref[pl.ds(off, KB_C), :]
            x = xb_ref[kb]
            sc = sf_ref[kb]
            sel = (jnp.where(x >= above_cb, one16, zero16).astype(F32) > 0.5) | (
                real & ((sc > tau) | ((sc == tau) & ((off + krow) <= jmax))))
            madd = jnp.where(sel, 0.0, NEG)
            s = jnp.concatenate([_dot_nt(kblk, q4[:half]), _dot_nt(kblk, q4[half:])], axis=1)
            scores.append(s + jnp.concatenate([madd] * HEADS, axis=1))
        m_new = m_old
        for s in scores:
            m_new = jnp.maximum(m_new, jnp.max(s, axis=0, keepdims=True))
        corr = jnp.exp(m_old - m_new)
        l_new = corr * l_old
        acc = corr * acc_old
        for kb, s in zip(blocks, scores):
            p = jnp.exp(s - m_new)
            l_new = l_new + jnp.sum(p, axis=0, keepdims=True)
            pb = p.astype(BF16)
            vtb = vt_ref[kb]
            acc = acc + jnp.concatenate([_dot(vtb, pb[:, :half]), _dot(vtb, pb[:, half:])], axis=1)
        return m_new, l_new, acc

    cols = HEADS * TQ_C
    init = (jnp.full((1, cols), NEG, F32), jnp.zeros((1, cols), F32), jnp.zeros((WIDTH, cols), F32))
    pairs = nkb // 2
    state = lax.fori_loop(0, pairs, lambda j, c: attend([2 * j, 2 * j + 1], c), init)
    _, l_fin, acc_fin = lax.fori_loop(2 * pairs, nkb, lambda kb, c: attend([kb], c), state)
    ot = acc_fin / l_fin
    out = jnp.zeros((TQ_C, WIDTH), F32)
    for h in range(HEADS):
        out = jnp.where((lane256 // HEAD_DIM) == h, ot[:, h * TQ_C:(h + 1) * TQ_C].T, out)
    o_ref[...] = out


def _dsa(p, qr, kr, vt, qir, kir, b, s):
    nt = s // TQ_C
    nkb = s // KB_C
    topk = min(TOPK_MAX, s // 4)
    idx_bits = int(math.log2(s)) + 1
    qrow = lambda w, col: pl.BlockSpec((TQ_C, w), lambda bb, i: (bb * nt + i, col))
    full = lambda w: pl.BlockSpec((s, w), lambda bb, i: (bb, 0))
    return pl.pallas_call(
        functools.partial(_dsa_kernel, topk=topk, idx_bits=idx_bits),
        grid=(b, nt),
        in_specs=[qrow(WIDTH, 0), full(WIDTH), pl.BlockSpec((nkb, WIDTH, KB_C), lambda bb, i: (bb, 0, 0)),
                  qrow(512, 0), full(128), qrow(128, H_CKW // 128)],
        out_specs=qrow(WIDTH, 0),
        out_shape=jax.ShapeDtypeStruct((b * s, WIDTH), F32),
        scratch_shapes=[
            pltpu.VMEM((nkb, KB_C, TQ_C), COARSE),
            pltpu.VMEM((nkb, KB_C, TQ_C), F32),
        ],
        compiler_params=_cparams(("parallel", "arbitrary")),
        name="dsa",
    )(qr, kr, vt, qir, kir, p)


TM_D = 256
TS_D = 256
NB_D = 4
assert TS_D == WIDTH and CHUNK == HEAD_DIM


def _dprep_kernel(r_ref, k_ref, v_ref, wa_ref, rp_ref, kp_ref, vp_ref, wap_ref,
                  mur_ref, muk_ref, muv_ref, muwa_ref, w0_ref, w2_ref, a0_ref, kkk_ref, ka_ref,
                  ro_ref, lwo_ref, kto_ref, vo_ref, kko_ref, ao_ref, *, seq):
    first = (pl.program_id(0) * TM_D) % seq == 0

    def shift(x_ref, p_ref, mu_ref):
        x = x_ref[...]
        row0 = jnp.where(first, 0.0, p_ref[7:8, :])
        rid = lax.broadcasted_iota(jnp.int32, x.shape, 0)
        prev = jnp.where(rid == 0, row0, pltpu.roll(x, 1, 0))
        return x + mu_ref[...] * (prev - x)

    r = shift(r_ref, rp_ref, mur_ref)
    k = shift(k_ref, kp_ref, muk_ref)
    v = shift(v_ref, vp_ref, muv_ref)
    wa = shift(wa_ref, wap_ref, muwa_ref)
    lane = lax.broadcasted_iota(jnp.int32, wa.shape, 1)
    lora_in = jnp.where(lane < LORA, jnp.tanh(wa), wa).astype(BF16)
    lora = _dot(lora_in, w2_ref[...])
    z = w0_ref[...] + lora[:, :WIDTH]
    wlog = -(jnp.maximum(-z, 0.0) + jnp.log1p(jnp.exp(-jnp.abs(z)))) - 0.5
    a = _sigmoid(a0_ref[...] + lora[:, WIDTH:])
    ro_ref[...] = r
    lwo_ref[...] = -jnp.exp(wlog)
    kto_ref[...] = k * (1.0 + (a - 1.0) * ka_ref[...])
    vo_ref[...] = v
    kko_ref[...] = k * kkk_ref[...]
    ao_ref[...] = a


def _dprep(p, mu, w0, w2, a0, kkk, ka, b, s):
    t = b * s
    rb = TM_D // 8
    c = H_D // WIDTH
    row = lambda w, col: pl.BlockSpec((TM_D, w), lambda i: (i, col))
    prev = lambda w, col: pl.BlockSpec((8, w), lambda i: (jnp.maximum(i * rb - 1, 0), col))
    vec = lambda w: pl.BlockSpec((1, w), lambda i: (0, 0))
    mur, muk, muv = mu[:, 0:256], mu[:, 256:512], mu[:, 512:768]
    muwa = mu[:, 768:896]
    return pl.pallas_call(
        functools.partial(_dprep_kernel, seq=s),
        grid=(t // TM_D,),
        in_specs=[row(WIDTH, c), row(WIDTH, c + 1), row(WIDTH, c + 2), row(128, H_DWA // 128),
                  prev(WIDTH, c), prev(WIDTH, c + 1), prev(WIDTH, c + 2), prev(128, H_DWA // 128),
                  vec(WIDTH), vec(WIDTH), vec(WIDTH), vec(128), vec(WIDTH),
                  pl.BlockSpec((128, 2 * WIDTH), lambda i: (0, 0)), vec(WIDTH), vec(WIDTH), vec(WIDTH)],
        out_specs=[row(WIDTH, 0)] * 6,
        out_shape=[jax.ShapeDtypeStruct((t, WIDTH), F32)] * 6,
        compiler_params=_cparams(("parallel",)),
        name="rwkv_prep",
    )(p, p, p, p, p, p, p, p, mur, muk, muv, muwa, w0, w2, a0, kkk, ka)


def _rwkv_kernel(r_ref, lw_ref, kt_ref, v_ref, kk_ref, a_ref, rk_ref, gg_ref, gb_ref, o_ref, st_ref):
    @pl.when(pl.program_id(1) == 0)
    def _():
        st_ref[...] = jnp.zeros(st_ref.shape, F32)

    steps = [_rwkv_step(*(x.at[bi] for x in (r_ref, lw_ref, kt_ref, v_ref, kk_ref, a_ref)), rk_ref, gg_ref, gb_ref,
                        o_ref.at[bi], st_ref.at[bi]) for bi in range(r_ref.shape[0])]
    for _ in itertools.zip_longest(*steps):
        pass


def _rwkv_step(r_ref, lw_ref, kt_ref, v_ref, kk_ref, a_ref, rk_ref, gg_ref, gb_ref, o_ref, st_ref):
    tt = lax.broadcasted_iota(jnp.int32, (TS_D, TS_D), 0)
    ss = lax.broadcasted_iota(jnp.int32, (TS_D, TS_D), 1)
    same = (tt // CHUNK) == (ss // CHUNK)
    strict = same & (tt > ss)
    incl = same & (tt >= ss)
    eye = jnp.where(tt == ss, 1.0, 0.0)
    ones_bd = jnp.where(same, 1.0, 0.0).astype(BF16)
    ltri_bd = jnp.where(incl, 1.0, 0.0).astype(BF16)
    hid = lax.broadcasted_iota(jnp.int32, (1, WIDTH), 1) // HEAD_DIM

    def group_sum(x):
        return _dot3(x, ones_bd, left=True)

    rr, lw, kt, vv, kkr, aa = (x[...] for x in (r_ref, lw_ref, kt_ref, v_ref, kk_ref, a_ref))
    kk = kkr / jnp.maximum(jnp.sqrt(group_sum(kkr * kkr)), 1e-12)
    be = kk * aa
    b = _dot3(ltri_bd, lw, left=False)
    bl = _dot3(ones_bd, lw, left=False)
    eb = jnp.exp(b)
    enb = jnp.exp(-b)
    dec = jnp.exp(bl - b)
    qa = -kk * jnp.exp(b - lw)
    qr = rr * eb
    q2 = jnp.concatenate([qa, qr], axis=0)
    k2 = jnp.concatenate([be * enb, kt * enb], axis=0).astype(BF16)
    zero = jnp.zeros((TS_D, WIDTH), F32)
    sa0, qt, qo, o0 = zero, zero, zero, zero
    for h in range(HEADS):
        hm = hid == h
        g = _dot_nt(jnp.where(hm, q2, 0.0).astype(BF16), k2)
        a_ab = jnp.where(strict, g[:TS_D, :TS_D], 0.0)
        a_ak = jnp.where(strict, g[:TS_D, TS_D:], 0.0).astype(BF16)
        a_rb = jnp.where(incl, g[TS_D:, :TS_D], 0.0).astype(BF16)
        a_rk = jnp.where(incl, g[TS_D:, TS_D:], 0.0).astype(BF16)
        tm = eye + a_ab
        pw = a_ab.astype(BF16)
        for _ in range(int(math.log2(CHUNK)) - 1):
            pwf = _dot(pw, pw)
            yield
            pw = pwf.astype(BF16)
            tm = tm + _dot(tm.astype(BF16), pw)
            yield
        tmb = tm.astype(BF16)
        vh = jnp.where(hm, vv, 0.0).astype(BF16)
        sa0_h = _dot(tmb, _dot(a_ak, vh).astype(BF16))
        yield
        qt_h = _dot(tmb, jnp.where(hm, qa, 0.0).astype(BF16))
        yield
        sa0 = sa0 + sa0_h
        qt = qt + qt_h
        qo = qo + jnp.where(hm, qr, 0.0) + _dot(a_rb, qt_h.astype(BF16))
        o0 = o0 + _dot(a_rb, sa0_h.astype(BF16)) + _dot(a_rk, vh)
        yield
    bd = be * dec
    kd = kt * dec
    outs = []
    for c in range(TS_D // CHUNK):
        rows = slice(c * CHUNK, (c + 1) * CHUNK)
        st = st_ref[...]
        stb = st.astype(BF16)
        sa_c = sa0[rows] + _dot_nt(qt[rows].astype(BF16), stb)
        outs.append(o0[rows] + _dot_nt(qo[rows].astype(BF16), stb))
        lhs_t = jnp.concatenate([sa_c, vv[rows]], axis=0).T.astype(BF16)
        rhs = jnp.concatenate([bd[rows], kd[rows]], axis=0).astype(BF16)
        upd = _dot(lhs_t, rhs)
        st_ref[...] = st * jnp.exp(bl[c * CHUNK:c * CHUNK + 1, :]) + jnp.where(same, upd, 0.0)
        yield
    o = jnp.concatenate(outs, axis=0)
    mean = group_sum(o) * (1.0 / HEAD_DIM)
    cen = o - mean
    var = group_sum(cen * cen) * (1.0 / HEAD_DIM)
    o = cen * lax.rsqrt(var + GN_EPS) * gg_ref[...] + gb_ref[...]
    o_ref[...] = o + group_sum(rr * kt * rk_ref[...]) * vv


def _rwkv7(r, lw, kt, v, kk, a, rk, gg, gb, b, s):
    nt = s // TS_D
    nb = NB_D if b % NB_D == 0 else 1
    row = pl.BlockSpec((nb, TS_D, WIDTH), lambda bb, i: (bb, i, 0))
    vec = pl.BlockSpec((1, WIDTH), lambda bb, i: (0, 0))
    seqs = [x.reshape(b, s, WIDTH) for x in (r, lw, kt, v, kk, a)]
    out = pl.pallas_call(
        _rwkv_kernel,
        grid=(b // nb, nt),
        in_specs=[row] * 6 + [vec] * 3,
        out_specs=row,
        out_shape=jax.ShapeDtypeStruct((b, s, WIDTH), F32),
        scratch_shapes=[pltpu.VMEM((nb, WIDTH, WIDTH), F32)],
        compiler_params=_cparams(("parallel", "arbitrary")),
        name="rwkv7",
    )(*seqs, rk, gg, gb)
    return out.reshape(b * s, WIDTH)


TM_M = 512


def _merge_kernel(oa_ref, ob_ref, oc_ref, od_ref, za_ref, zb_ref, zc_ref, zd_ref,
                  ga_ref, gb_ref, gc_ref, gd_ref, x_ref, wb_ref, wo_ref, fg_ref, o_ref, *, final):
    def branch(o_r, z_r, g_r, idx):
        z = z_r[...].astype(F32)
        u = (o_r[...] * (z * _sigmoid(z))).astype(BF16)
        return _sigmoid(g_r[...].astype(F32)) * _dot(u, wb_ref[idx])

    y = (branch(oa_ref, za_ref, ga_ref, 0) + branch(ob_ref, zb_ref, gb_ref, 1)
         + branch(oc_ref, zc_ref, gc_ref, 2) + branch(od_ref, zd_ref, gd_ref, 3))
    xn = x_ref[...] + _dot(y.astype(BF16), wo_ref[...])
    if final:
        inv = lax.rsqrt(jnp.mean(xn * xn, axis=-1, keepdims=True) + EPS)
        xn = (xn * inv) * fg_ref[...]
    o_ref[...] = xn


def _merge(oa, ob, oc, od, p, x2, wb, wo, fg, final):
    t = x2.shape[0]
    row = lambda w, col: pl.BlockSpec((TM_M, w), lambda i: (i, col))
    zcols = [(L_A + 3 * WIDTH) // WIDTH, (L_BQ + 2 * WIDTH) // WIDTH, (L_C + 3 * WIDTH) // WIDTH, L_DZ // WIDTH]
    return pl.pallas_call(
        functools.partial(_merge_kernel, final=final),
        grid=(t // TM_M,),
        in_specs=[row(WIDTH, 0)] * 4
        + [row(WIDTH, zc) for zc in zcols]
        + [row(D_MODEL, L_G // D_MODEL + n) for n in range(4)]
        + [row(D_MODEL, 0),
           pl.BlockSpec((4, WIDTH, D_MODEL), lambda i: (0, 0, 0)),
           pl.BlockSpec((D_MODEL, D_MODEL), lambda i: (0, 0)),
           pl.BlockSpec((1, D_MODEL), lambda i: (0, 0))],
        out_specs=row(D_MODEL, 0),
        out_shape=jax.ShapeDtypeStruct((t, D_MODEL), F32),
        compiler_params=_cparams(("parallel",)),
        name="merge",
    )(oa, ob, oc, od, p, p, p, p, p, p, p, p, x2, wb, wo, fg)


def _split_w_in(w):
    def seg(lo, hi, width):
        return jnp.pad(w[:, lo:hi], ((0, 0), (0, width - (hi - lo))))
    c_idx, c_end = 2048 + 4 * WIDTH, 2048 + 1608
    d_end = c_end + 896
    b_f = 1024 + WIDTH
    w_hi = jnp.concatenate([seg(c_idx, c_end, H_BF), w[:, b_f:b_f + WIDTH], seg(c_end, d_end, NPH - H_D)], axis=1)
    w_lo = jnp.concatenate([w[:, :b_f], w[:, b_f + WIDTH:c_idx], w[:, d_end:]], axis=1)
    return w_hi.astype(BF16), w_lo.astype(BF16)


def kernel(x, norm_g, w_in, a_rel_bias, b_lb_logits, b_norm_g, d_mu, d_w0, d_w_up, d_a0, d_a_up,
           d_k_k, d_k_a, d_r_k, d_gn_g, d_gn_b, w_branch, w_out, final_g):
    b, s, d = x.shape
    depth = norm_g.shape[0]
    t = b * s
    x2 = x.reshape(t, d)
    cos, sin = _rope_tables(s)
    lb_all = jnp.cumsum(jax.nn.softmax(b_lb_logits.astype(F32), axis=0), axis=0)
    lb_all = lb_all - lb_all[0:1]
    tm_in = next(m for m in (1024, 512, 256) if t % m == 0)
    for l in range(depth):
        w_hi, w_lo = _split_w_in(w_in[l])
        ph = _inproj(x2, norm_g[l][None, :], w_hi, tm_in, NPH, F32)
        p16 = _inproj(x2, norm_g[l][None, :], w_lo, tm_in, NPL // 2, BF16)
        oa = _band_attention(p16, _band_bias(a_rel_bias[l]), b, s)
        lb = lb_all[l][None, :]
        ob = _hgrn2(ph, p16, jnp.log(lb), jnp.log1p(-lb), 1.0 - lb, b_norm_g[l][None, :], b, s)
        qr, kr, vt, qir, kir = _rope_prep(ph, p16, cos, sin, b, s)
        oc = _dsa(ph, qr, kr, vt, qir, kir, b, s)
        zeros = jnp.zeros((LORA, WIDTH), F32)
        w2 = jnp.concatenate([jnp.concatenate([d_w_up[l], zeros], axis=1),
                              jnp.concatenate([zeros, d_a_up[l]], axis=1)], axis=0).astype(BF16)
        r, lw, kt, v, kk, a = _dprep(ph, d_mu[l][None, :], d_w0[l][None, :], w2, d_a0[l][None, :],
                                     d_k_k[l][None, :], d_k_a[l][None, :], b, s)
        od = _rwkv7(r, lw, kt, v, kk, a, d_r_k[l].reshape(1, WIDTH), d_gn_g[l][None, :], d_gn_b[l][None, :], b, s)
        x2 = _merge(oa, ob, oc, od, p16, x2, w_branch[l].astype(BF16), w_out[l].astype(BF16),
                    final_g[None, :], l == depth - 1)
    return x2.reshape(b, s, d)
```

```python
import functools
import itertools
import math

import numpy as np
import jax
import jax.numpy as jnp
from jax import lax
from jax.experimental import pallas as pl
from jax.experimental.pallas import tpu as pltpu

F32 = jnp.float32
BF16 = jnp.bfloat16

D_MODEL = 1024
HEADS = 4
HEAD_DIM = 64
WIDTH = HEADS * HEAD_DIM
CHUNK = 64
A_LEFT_CHUNKS = 8
REL_CLIP = 128
IDX_HEADS = 8
IDX_DIM = 64
TOPK_MAX = 256
LORA = 64
ROPE_THETA = 10000.0
EPS = 1e-6
GN_EPS = 64e-5
NEG = -1e30
INT_MIN = -2 ** 31
I16 = jnp.int16
I16_MIN = -2 ** 15
COARSE = jnp.bfloat16
KEY_NEG_INF = INT_MIN + 0x7FFFFF

H_CQI = 0
H_CKW = 512
H_BF = 768
H_D = 1024
H_DWA = 1792
NPH = 2048
L_A = 0
L_BQ = 1024
L_C = 1792
L_DZ = 2816
L_G = 3072
NPL = 7168
VMEM_LIMIT = 56 * 1024 * 1024

_NT = (((1,), (1,)), ((), ()))


def _sigmoid(x):
    return 1.0 / (1.0 + jnp.exp(-x))


def _dot(a, b, precision=None):
    return jnp.dot(a, b, preferred_element_type=F32, precision=precision)


def _dot_nt(a, b, precision=None):
    return lax.dot_general(a, b, _NT, preferred_element_type=F32, precision=precision)


def _dot3(a, b, left):
    x = a if left else b
    hi = x.astype(BF16)
    rest = x - hi.astype(F32)
    mid = rest.astype(BF16)
    lo = (rest - mid.astype(F32)).astype(BF16)
    if left:
        return _dot(hi, b) + _dot(mid, b) + _dot(lo, b)
    return _dot(a, hi) + _dot(a, mid) + _dot(a, lo)


def _cparams(sem):
    return pltpu.CompilerParams(dimension_semantics=sem, vmem_limit_bytes=VMEM_LIMIT)


def _inproj_kernel(x_ref, g_ref, w_ref, o_ref, h_ref):
    @pl.when(pl.program_id(1) == 0)
    def _():
        x = x_ref[...]
        inv = lax.rsqrt(jnp.mean(x * x, axis=-1, keepdims=True) + EPS)
        h_ref[...] = ((x * inv) * g_ref[...]).astype(BF16)

    o_ref[...] = _dot(h_ref[...], w_ref[...]).astype(o_ref.dtype)


def _inproj(x2, g, w, tm, tn, dtype):
    t = x2.shape[0]
    n = w.shape[1]
    return pl.pallas_call(
        _inproj_kernel,
        grid=(t // tm, n // tn),
        in_specs=[
            pl.BlockSpec((tm, D_MODEL), lambda i, j: (i, 0)),
            pl.BlockSpec((1, D_MODEL), lambda i, j: (0, 0)),
            pl.BlockSpec((D_MODEL, tn), lambda i, j: (0, j)),
        ],
        out_specs=pl.BlockSpec((tm, tn), lambda i, j: (i, j)),
        out_shape=jax.ShapeDtypeStruct((t, n), dtype),
        scratch_shapes=[pltpu.VMEM((tm, D_MODEL), BF16)],
        compiler_params=_cparams(("parallel", "arbitrary")),
        name="inproj",
    )(x2, g, w)


TQ_A = 256
WIN_A = 3 * TQ_A


def _band_kernel(q_ref, k0_ref, k1_ref, k2_ref, v0_ref, v1_ref, v2_ref, bias_ref, o_ref):
    i = pl.program_id(1)
    q = q_ref[...]
    kc = jnp.concatenate([k0_ref[...], k1_ref[...], k2_ref[...]], axis=0).astype(BF16)
    vc = jnp.concatenate([v0_ref[...], v1_ref[...], v2_ref[...]], axis=0).astype(BF16)
    lane = lax.broadcasted_iota(jnp.int32, (1, WIDTH), 1)
    kpos = lax.broadcasted_iota(jnp.int32, (1, WIN_A), 1) + (i - 2) * TQ_A
    kvalid = kpos >= 0
    out = jnp.zeros((TQ_A, WIDTH), F32)
    for h in range(HEADS):
        hm = (lane // HEAD_DIM) == h
        qh = jnp.where(hm, q, jnp.zeros((), q.dtype)).astype(BF16)
        s = _dot_nt(qh, kc) * (HEAD_DIM ** -0.5) + bias_ref[h]
        s = jnp.where(kvalid, s, NEG)
        m = jnp.max(s, axis=-1, keepdims=True)
        p = jnp.exp(s - m)
        l = jnp.sum(p, axis=-1, keepdims=True)
        o = _dot(p.astype(BF16), vc) / l
        out = jnp.where(hm, o, out)
    o_ref[...] = out


def _band_attention(p, bias, b, s):
    nt = s // TQ_A
    qcol, kcol, vcol = L_A // WIDTH, L_A // WIDTH + 1, L_A // WIDTH + 2

    def kv_spec(col, back):
        return pl.BlockSpec((TQ_A, WIDTH), lambda bb, i: (bb * nt + jnp.maximum(i - back, 0), col))

    return pl.pallas_call(
        _band_kernel,
        grid=(b, nt),
        in_specs=[
            pl.BlockSpec((TQ_A, WIDTH), lambda bb, i: (bb * nt + i, qcol)),
            kv_spec(kcol, 2), kv_spec(kcol, 1), kv_spec(kcol, 0),
            kv_spec(vcol, 2), kv_spec(vcol, 1), kv_spec(vcol, 0),
            pl.BlockSpec((HEADS, TQ_A, WIN_A), lambda bb, i: (0, 0, 0)),
        ],
        out_specs=pl.BlockSpec((TQ_A, WIDTH), lambda bb, i: (bb * nt + i, 0)),
        out_shape=jax.ShapeDtypeStruct((b * s, WIDTH), F32),
        compiler_params=_cparams(("parallel", "parallel")),
        name="band_attention",
    )(p, p, p, p, p, p, p, bias)


def _band_bias(rel_bias):
    tab = rel_bias.astype(F32)
    n = tab.shape[1]
    d_lo, d_hi = 2 * TQ_A - (WIN_A - 1), 2 * TQ_A + TQ_A - 1
    n_lo = -(CHUNK - 1) - d_lo + 1
    n_hi = d_hi - REL_CLIP + 1
    v = jnp.concatenate([jnp.repeat(tab[:, :1], n_lo, axis=1), tab[:, 1:n - 1],
                         jnp.repeat(tab[:, n - 1:], n_hi, axis=1)], axis=1)
    span = WIN_A + TQ_A
    u = jnp.concatenate([v[:, :WIN_A][:, ::-1], jnp.zeros((HEADS, span - v.shape[1]), F32),
                         v[:, WIN_A:][:, ::-1]], axis=1)
    skew = jnp.tile(u, (1, TQ_A))[:, :TQ_A * (span - 1)].reshape(HEADS, TQ_A, span - 1)[:, :, :WIN_A]
    ql = np.arange(TQ_A)[:, None]
    kl = np.arange(WIN_A)[None, :]
    lo = CHUNK * (ql // CHUNK)
    band = (kl >= lo) & (kl < lo + (A_LEFT_CHUNKS + 1) * CHUNK)
    return jnp.where(band[None], skew, NEG)


TS_B = 256
NB_B = 4
_LEVELS = (1, 2, 4, 8, 16, 32)
assert TS_B == WIDTH and CHUNK == HEAD_DIM


def _hgrn_kernel(q_ref, f_ref, i_ref, la_ref, l1_ref, oml_ref, ng_ref, o_ref, st_ref):
    @pl.when(pl.program_id(1) == 0)
    def _():
        st_ref[...] = jnp.zeros(st_ref.shape, F32)

    steps = [_hgrn_step(q_ref.at[bi], f_ref.at[bi], i_ref.at[bi], la_ref, l1_ref, oml_ref, ng_ref,
                        o_ref.at[bi], st_ref.at[bi]) for bi in range(q_ref.shape[0])]
    for _ in itertools.zip_longest(*steps):
        pass


def _hgrn_step(q_ref, f_ref, i_ref, la_ref, l1_ref, oml_ref, ng_ref, o_ref, st_ref):
    fl = f_ref[...]
    logsig = jnp.minimum(fl, 0.0) - jnp.log1p(jnp.exp(-jnp.abs(fl)))
    y = l1_ref[...] + logsig
    la = la_ref[...]
    g = jnp.maximum(la, y) + jnp.log1p(jnp.exp(-jnp.abs(la - y)))
    kk = oml_ref[...] * _sigmoid(-fl)
    qq = q_ref[...].astype(F32)
    qf = qq * _sigmoid(qq)
    vv = i_ref[...].astype(F32)
    tt = lax.broadcasted_iota(jnp.int32, (TS_B, TS_B), 0)
    ss = lax.broadcasted_iota(jnp.int32, (TS_B, TS_B), 1)
    same = (tt // CHUNK) == (ss // CHUNK)
    ones_bd = jnp.where(same, 1.0, 0.0).astype(BF16)
    ltri_bd = jnp.where(same & (tt >= ss), 1.0, 0.0).astype(BF16)
    hid = lax.broadcasted_iota(jnp.int32, (1, WIDTH), 1) // HEAD_DIM
    heads4 = lambda x: jnp.concatenate([jnp.where(hid == h, x, 0.0) for h in range(HEADS)], axis=0).astype(BF16)
    tile4 = lambda mask: jnp.concatenate([mask] * HEADS, axis=0)

    b = _dot3(ltri_bd, g, left=False)
    bl = _dot3(ones_bd, g, left=False)
    att4 = jnp.where(tile4(tt == ss), _dot_nt(heads4(qf), kk.astype(BF16)), 0.0)
    for m in _LEVELS:
        pick = jnp.where(ss == (tt // (2 * m)) * (2 * m) + (m - 1), 1.0, 0.0).astype(BF16)
        b_hi = b.astype(BF16)
        br = _dot(pick, b_hi) + _dot(pick, (b - b_hi.astype(F32)).astype(BF16))
        ql = qf * jnp.exp(jnp.minimum(b - br, 1.0))
        kl = kk * jnp.exp(jnp.minimum(br - b, 1.0))
        mask = ((tt // (2 * m)) == (ss // (2 * m))) & ((tt % (2 * m)) >= m) & ((ss % (2 * m)) < m)
        att4 = att4 + jnp.where(tile4(mask), _dot_nt(heads4(ql), kl.astype(BF16)), 0.0)
        yield
    att4 = att4.astype(BF16)
    o = jnp.zeros((TS_B, WIDTH), F32)
    for h in range(HEADS):
        o = o + _dot(att4[h * TS_B:(h + 1) * TS_B], jnp.where(hid == h, vv, 0.0).astype(BF16))
        yield
    qe = (qf * jnp.exp(b)).astype(BF16)
    kd = (kk * jnp.exp(bl - b)).astype(BF16)
    outs = []
    for c in range(TS_B // CHUNK):
        rows = slice(c * CHUNK, (c + 1) * CHUNK)
        st = st_ref[...]
        outs.append(_dot_nt(qe[rows], st.astype(BF16)))
        upd = _dot(vv[rows].T.astype(BF16), kd[rows])
        st_ref[...] = st * jnp.exp(bl[c * CHUNK:c * CHUNK + 1, :]) + jnp.where(same, upd, 0.0)
        yield
    o = o + jnp.concatenate(outs, axis=0)
    ms = _dot3(o * o, ones_bd, left=True) * (1.0 / HEAD_DIM)
    o_ref[...] = o * lax.rsqrt(ms + EPS) * ng_ref[...]


def _hgrn2(ph, pl16, la, l1, oml, ng, b, s):
    nt = s // TS_B
    nb = NB_B if b % NB_B == 0 else 1
    row = lambda col: pl.BlockSpec((nb, TS_B, WIDTH), lambda bb, i: (bb, i, col))
    vec = pl.BlockSpec((1, WIDTH), lambda bb, i: (0, 0))
    ph3 = ph.reshape(b, s, NPH)
    pl3 = pl16.reshape(b, s, NPL)
    out = pl.pallas_call(
        _hgrn_kernel,
        grid=(b // nb, nt),
        in_specs=[row(L_BQ // WIDTH), row(H_BF // WIDTH), row(L_BQ // WIDTH + 1), vec, vec, vec, vec],
        out_specs=row(0),
        out_shape=jax.ShapeDtypeStruct((b, s, WIDTH), F32),
        scratch_shapes=[pltpu.VMEM((nb, WIDTH, WIDTH), F32)],
        compiler_params=_cparams(("parallel", "arbitrary")),
        name="hgrn2",
    )(pl3, ph3, pl3, la, l1, oml, jnp.tile(ng, (1, HEADS)))
    return out.reshape(b * s, WIDTH)


TQ_C = 512
KB_C = 512
TM_R = KB_C


def _rope(x, cos, sin_signed):
    w = x.shape[-1]
    lane = lax.broadcasted_iota(jnp.int32, x.shape, 1)
    low = (lane % HEAD_DIM) < (HEAD_DIM // 2)
    swapped = jnp.where(low, pltpu.roll(x, w - HEAD_DIM // 2, 1), pltpu.roll(x, HEAD_DIM // 2, 1))
    return x * cos + swapped * sin_signed


def _rope_kernel(q_ref, k_ref, v_ref, qi_ref, kw_ref, cos_ref, sin_ref, qo_ref, ko_ref, vo_ref, qio_ref, kio_ref):
    cos = cos_ref[...]
    sin = sin_ref[...]
    qo_ref[...] = _rope(q_ref[...].astype(F32), cos, sin).astype(BF16)
    ko_ref[...] = _rope(k_ref[...].astype(F32), cos, sin).astype(BF16)
    vo_ref[0] = v_ref[...].astype(F32).T.astype(BF16)
    cos2 = jnp.concatenate([cos, cos], axis=1)
    sin2 = jnp.concatenate([sin, sin], axis=1)
    qio_ref[...] = _rope(qi_ref[...], cos2, sin2).astype(BF16)
    kr = _rope(kw_ref[...], cos[:, :128], sin[:, :128])
    lane = lax.broadcasted_iota(jnp.int32, kr.shape, 1)
    kio_ref[...] = jnp.where(lane < IDX_DIM, kr, pltpu.roll(kr, IDX_DIM, 1)).astype(BF16)


def _rope_prep(ph, pl16, cos, sin, b, s):
    t = b * s
    nt = s // TM_R
    c = L_C // WIDTH
    row = lambda w, col: pl.BlockSpec((TM_R, w), lambda i: (i, col))
    tab = pl.BlockSpec((TM_R, WIDTH), lambda i: (i % nt, 0))
    return pl.pallas_call(
        _rope_kernel,
        grid=(t // TM_R,),
        in_specs=[row(WIDTH, c), row(WIDTH, c + 1), row(WIDTH, c + 2),
                  row(512, H_CQI // 512), row(128, H_CKW // 128), tab, tab],
        out_specs=[row(WIDTH, 0), row(WIDTH, 0), pl.BlockSpec((1, WIDTH, TM_R), lambda i: (i, 0, 0)),
                   row(512, 0), row(128, 0)],
        out_shape=[jax.ShapeDtypeStruct((t, WIDTH), BF16)] * 2
        + [jax.ShapeDtypeStruct((t // TM_R, WIDTH, TM_R), BF16),
           jax.ShapeDtypeStruct((t, 512), BF16), jax.ShapeDtypeStruct((t, 128), BF16)],
        compiler_params=_cparams(("parallel",)),
        name="rope_prep",
    )(pl16, pl16, pl16, ph, ph, cos, sin)


def _rope_tables(s):
    half = HEAD_DIM // 2
    freqs = ROPE_THETA ** (-jnp.arange(half, dtype=F32) / half)
    ang = jnp.arange(s, dtype=jnp.int32).astype(F32)[:, None] * freqs[None, :]
    cos = jnp.cos(ang)
    sin = jnp.sin(ang)
    cos = jnp.tile(jnp.concatenate([cos, cos], axis=1), (1, HEADS))
    sin = jnp.tile(jnp.concatenate([-sin, sin], axis=1), (1, HEADS))
    return cos, sin


def _dsa_kernel(q_ref, k_ref, vt_ref, qi_ref, ki_ref, w_ref, o_ref, xb_ref, sf_ref, *, topk, idx_bits):
    i = pl.program_id(1)
    nkb = ((i + 1) * TQ_C + KB_C - 1) // KB_C
    lane128 = lax.broadcasted_iota(jnp.int32, (1, 128), 1)
    lane256 = lax.broadcasted_iota(jnp.int32, (1, WIDTH), 1)
    krow = lax.broadcasted_iota(jnp.int32, (KB_C, 1), 0)
    qchunk = (i * TQ_C + lax.broadcasted_iota(jnp.int32, (1, TQ_C), 1)) // CHUNK
    wt = (w_ref[...] * (IDX_HEADS ** -0.5)).T
    qi = qi_ref[...] * jnp.asarray(IDX_DIM ** -0.5, BF16)
    qi_heads = [jnp.where((lane128 // IDX_DIM) == (h % 2), qi[:, 128 * (h // 2):128 * (h // 2 + 1)],
                          jnp.zeros((), BF16)) for h in range(IDX_HEADS)]

    def score_block(kb, carry):
        off = pl.multiple_of(kb * KB_C, KB_C)
        kib = ki_ref[pl.ds(off, KB_C), :]
        acc = jnp.zeros((KB_C, TQ_C), F32)
        for h in range(IDX_HEADS):
            acc = acc + jnp.maximum(_dot_nt(kib, qi_heads[h]), 0.0) * wt[IDX_DIM + h:IDX_DIM + h + 1, :]
        adm = ((off + krow) // CHUNK) <= qchunk
        sc = jnp.where(adm, acc, -jnp.inf)
        sf_ref[kb] = sc
        xb_ref[kb] = sc.astype(COARSE)
        return carry

    lax.fori_loop(0, nkb, score_block, 0)

    def float_of_key(key):
        key = jnp.maximum(key, KEY_NEG_INF)
        return lax.bitcast_convert_type(key ^ ((key >> 31) & jnp.int32(0x7FFFFFFF)), F32)

    def coarse_of_key(key16):
        return float_of_key(jnp.where(key16 < 0, (key16 << 16) + 0xFFFF, key16 << 16))

    one16 = jnp.ones((), I16)
    zero16 = jnp.zeros((), I16)

    def count16(pred):
        def body(kb, acc):
            c3 = jnp.where(pred(xb_ref[kb]), one16, zero16).reshape(KB_C // 16, 16, TQ_C)
            return acc + functools.reduce(lambda a, b: a + b, [c3[g] for g in range(KB_C // 16)])
        acc = lax.fori_loop(0, nkb, body, jnp.zeros((16, TQ_C), I16))
        return jnp.sum(acc.astype(F32), axis=0, keepdims=True)

    def count32(pred):
        def body(kb, acc):
            c3 = jnp.where(pred(kb, sf_ref[kb]), 1.0, 0.0).reshape(KB_C // 8, 8, TQ_C)
            return acc + functools.reduce(lambda a, b: a + b, [c3[g] for g in range(KB_C // 8)])
        acc = lax.fori_loop(0, nkb, body, jnp.zeros((8, TQ_C), F32))
        return jnp.sum(acc, axis=0, keepdims=True)

    def coarse_bit(bi, t):
        cand = t + lax.shift_left(jnp.int32(1), 15 - bi)
        cb = coarse_of_key(cand).astype(COARSE)
        return jnp.where(count16(lambda x: x >= cb) >= topk, cand, t)

    t16 = lax.fori_loop(0, 16, coarse_bit, jnp.full((1, TQ_C), I16_MIN, jnp.int32))
    tau_c = coarse_of_key(t16)
    next_c = coarse_of_key(t16 + 1)
    tau_cb = tau_c.astype(COARSE)
    next_cb = next_c.astype(COARSE)
    real = tau_c > -jnp.inf
    above_cb = jnp.where(real, next_c, float(jnp.finfo(COARSE).min)).astype(COARSE)
    kth_fine = topk - count16(lambda x: x >= next_cb)

    def mask_bucket(kb, carry):
        x = xb_ref[kb]
        inside = jnp.where((x >= tau_cb) & jnp.logical_not(x >= next_cb), one16, zero16).astype(F32)
        sf_ref[kb] = jnp.where(inside > 0.5, sf_ref[kb], -jnp.inf)
        return carry

    lax.fori_loop(0, nkb, mask_bucket, 0)

    t16s = jnp.where(t16 < 0, (t16 << 16) + 0xFFFF, t16 << 16)

    def fine_bit(bi, t):
        cand = t + lax.shift_left(jnp.int32(1), 16 - bi)
        cf = float_of_key(cand)
        return jnp.where(count32(lambda kb, sc: sc >= cf) >= kth_fine, cand, t)

    tau = float_of_key(lax.fori_loop(0, 17, fine_bit, t16s - 0x8000))
    need = kth_fine - count32(lambda kb, sc: sc > tau)
    ties = count32(lambda kb, sc: sc == tau)

    def index_search():
        def index_bit(bi, jmax):
            cand = jmax + lax.shift_left(jnp.int32(1), idx_bits - 1 - bi)
            cnt = count32(lambda kb, sc: (sc == tau) & ((kb * KB_C + krow) < cand))
            return jnp.where(cnt < need, cand, jmax)
        return lax.fori_loop(0, idx_bits, index_bit, jnp.zeros((1, TQ_C), jnp.int32))

    excess = jnp.max(jnp.where((ties > need) & real, 1.0, 0.0)) > 0.5
    jmax = lax.cond(excess, index_search, lambda: jnp.full((1, TQ_C), 2 ** (idx_bits - 1), jnp.int32))

    q = q_ref[...] * jnp.asarray(HEAD_DIM ** -0.5, BF16)
    q4 = jnp.concatenate([jnp.where((lane256 // HEAD_DIM) == h, q, jnp.zeros((), BF16)) for h in range(HEADS)],
                         axis=0)
    half = HEADS * TQ_C // 2

    def attend(blocks, carry):
        m_old, l_old, acc_old = carry
        scores = []
        for kb in blocks:
            off = pl.multiple_of(kb * KB_C, KB_C)
            kblk = k_ref[pl.ds(off, KB_C), :]
            x = xb_ref[kb]
            sc = sf_ref[kb]
            sel = (jnp.where(x >= above_cb, one16, zero16).astype(F32) > 0.5) | (
                real & ((sc > tau) | ((sc == tau) & ((off + krow) <= jmax))))
            madd = jnp.where(sel, 0.0, NEG)
            s = jnp.concatenate([_dot_nt(kblk, q4[:half]), _dot_nt(kblk, q4[half:])], axis=1)
            scores.append(s + jnp.concatenate([madd] * HEADS, axis=1))
        m_new = m_old
        for s in scores:
            m_new = jnp.maximum(m_new, jnp.max(s, axis=0, keepdims=True))
        corr = jnp.exp(m_old - m_new)
        l_new = corr * l_old
        acc = corr * acc_old
        for kb, s in zip(blocks, scores):
            p = jnp.exp(s - m_new)
            l_new = l_new + jnp.sum(p, axis=0, keepdims=True)
            pb = p.astype(BF16)
            vtb = vt_ref[kb]
            acc = acc + jnp.concatenate([_dot(vtb, pb[:, :half]), _dot(vtb, pb[:, half:])], axis=1)
        return m_new, l_new, acc

    cols = HEADS * TQ_C
    init = (jnp.full((1, cols), NEG, F32), jnp.zeros((1, cols), F32), jnp.zeros((WIDTH, cols), F32))
    pairs = nkb // 2
    state = lax.fori_loop(0, pairs, lambda j, c: attend([2 * j, 2 * j + 1], c), init)
    _, l_fin, acc_fin = lax.fori_loop(2 * pairs, nkb, lambda kb, c: attend([kb], c), state)
    ot = acc_fin / l_fin
    out = jnp.zeros((TQ_C, WIDTH), F32)
    for h in range(HEADS):
        out = jnp.where((lane256 // HEAD_DIM) == h, ot[:, h * TQ_C:(h + 1) * TQ_C].T, out)
    o_ref[...] = out


def _dsa(p, qr, kr, vt, qir, kir, b, s):
    nt = s // TQ_C
    nkb = s // KB_C
    topk = min(TOPK_MAX, s // 4)
    idx_bits = int(math.log2(s)) + 1
    qrow = lambda w, col: pl.BlockSpec((TQ_C, w), lambda bb, i: (bb * nt + i, col))
    full = lambda w: pl.BlockSpec((s, w), lambda bb, i: (bb, 0))
    return pl.pallas_call(
        functools.partial(_dsa_kernel, topk=topk, idx_bits=idx_bits),
        grid=(b, nt),
        in_specs=[qrow(WIDTH, 0), full(WIDTH), pl.BlockSpec((nkb, WIDTH, KB_C), lambda bb, i: (bb, 0, 0)),
                  qrow(512, 0), full(128), qrow(128, H_CKW // 128)],
        out_specs=qrow(WIDTH, 0),
        out_shape=jax.ShapeDtypeStruct((b * s, WIDTH), F32),
        scratch_shapes=[
            pltpu.VMEM((nkb, KB_C, TQ_C), COARSE),
            pltpu.VMEM((nkb, KB_C, TQ_C), F32),
        ],
        compiler_params=_cparams(("parallel", "arbitrary")),
        name="dsa",
    )(qr, kr, vt, qir, kir, p)


TS_D = 256
NB_D = 4
assert TS_D == WIDTH and CHUNK == HEAD_DIM


def _rwkv_kernel(r_ref, k_ref, v_ref, wa_ref, mur_ref, muk_ref, muv_ref, muwa_ref, w0_ref, w2_ref, a0_ref,
                 kkk_ref, ka_ref, rk_ref, gg_ref, gb_ref, o_ref, st_ref, pr_ref, pk_ref, pv_ref, pwa_ref):
    @pl.when(pl.program_id(1) == 0)
    def _():
        for ref in (st_ref, pr_ref, pk_ref, pv_ref, pwa_ref):
            ref[...] = jnp.zeros(ref.shape, F32)

    params = (mur_ref, muk_ref, muv_ref, muwa_ref, w0_ref, w2_ref, a0_ref, kkk_ref, ka_ref, rk_ref, gg_ref, gb_ref)
    steps = [_rwkv_step(*(x.at[bi] for x in (r_ref, k_ref, v_ref, wa_ref)), *params,
                        *(x.at[bi] for x in (o_ref, st_ref, pr_ref, pk_ref, pv_ref, pwa_ref)))
             for bi in range(r_ref.shape[0])]
    for _ in itertools.zip_longest(*steps):
        pass


def _rwkv_step(r_ref, k_ref, v_ref, wa_ref, mur_ref, muk_ref, muv_ref, muwa_ref, w0_ref, w2_ref, a0_ref,
               kkk_ref, ka_ref, rk_ref, gg_ref, gb_ref, o_ref, st_ref, pr_ref, pk_ref, pv_ref, pwa_ref):
    def shift(x_ref, p_ref, mu_ref):
        x = x_ref[...]
        rid = lax.broadcasted_iota(jnp.int32, x.shape, 0)
        prev = jnp.where(rid == 0, p_ref[0:1, :], pltpu.roll(x, 1, 0))
        p_ref[0:1, :] = x[TS_D - 1:TS_D, :]
        return x + mu_ref[...] * (prev - x)

    rr = shift(r_ref, pr_ref, mur_ref)
    kf = shift(k_ref, pk_ref, muk_ref)
    vv = shift(v_ref, pv_ref, muv_ref)
    wa = shift(wa_ref, pwa_ref, muwa_ref)
    lane_wa = lax.broadcasted_iota(jnp.int32, wa.shape, 1)
    lora = _dot(jnp.where(lane_wa < LORA, jnp.tanh(wa), wa).astype(BF16), w2_ref[...])
    zz = w0_ref[...] + lora[:, :WIDTH]
    lw = -jnp.exp(-(jnp.maximum(-zz, 0.0) + jnp.log1p(jnp.exp(-jnp.abs(zz)))) - 0.5)
    aa = _sigmoid(a0_ref[...] + lora[:, WIDTH:])
    kt = kf * (1.0 + (aa - 1.0) * ka_ref[...])
    kkr = kf * kkk_ref[...]
    yield
    tt = lax.broadcasted_iota(jnp.int32, (TS_D, TS_D), 0)
    ss = lax.broadcasted_iota(jnp.int32, (TS_D, TS_D), 1)
    same = (tt // CHUNK) == (ss // CHUNK)
    strict = same & (tt > ss)
    incl = same & (tt >= ss)
    eye = jnp.where(tt == ss, 1.0, 0.0)
    ones_bd = jnp.where(same, 1.0, 0.0).astype(BF16)
    ltri_bd = jnp.where(incl, 1.0, 0.0).astype(BF16)
    hid = lax.broadcasted_iota(jnp.int32, (1, WIDTH), 1) // HEAD_DIM

    def group_sum(x):
        return _dot3(x, ones_bd, left=True)

    kk = kkr / jnp.maximum(jnp.sqrt(group_sum(kkr * kkr)), 1e-12)
    be = kk * aa
    b = _dot3(ltri_bd, lw, left=False)
    bl = _dot3(ones_bd, lw, left=False)
    eb = jnp.exp(b)
    enb = jnp.exp(-b)
    dec = jnp.exp(bl - b)
    qa = -kk * jnp.exp(b - lw)
    qr = rr * eb
    q2 = jnp.concatenate([qa, qr], axis=0)
    k2 = jnp.concatenate([be * enb, kt * enb], axis=0).astype(BF16)
    zero = jnp.zeros((TS_D, WIDTH), F32)
    sa0, qt, qo, o0 = zero, zero, zero, zero
    for h in range(HEADS):
        hm = hid == h
        g = _dot_nt(jnp.where(hm, q2, 0.0).astype(BF16), k2)
        a_ab = jnp.where(strict, g[:TS_D, :TS_D], 0.0)
        a_ak = jnp.where(strict, g[:TS_D, TS_D:], 0.0).astype(BF16)
        a_rb = jnp.where(incl, g[TS_D:, :TS_D], 0.0).astype(BF16)
        a_rk = jnp.where(incl, g[TS_D:, TS_D:], 0.0).astype(BF16)
        tm = eye + a_ab
        pw = a_ab.astype(BF16)
        for _ in range(int(math.log2(CHUNK)) - 1):
            pwf = _dot(pw, pw)
            yield
            pw = pwf.astype(BF16)
            tm = tm + _dot(tm.astype(BF16), pw)
            yield
        tmb = tm.astype(BF16)
        vh = jnp.where(hm, vv, 0.0).astype(BF16)
        sa0_h = _dot(tmb, _dot(a_ak, vh).astype(BF16))
        yield
        qt_h = _dot(tmb, jnp.where(hm, qa, 0.0).astype(BF16))
        yield
        sa0 = sa0 + sa0_h
        qt = qt + qt_h
        qo = qo + jnp.where(hm, qr, 0.0) + _dot(a_rb, qt_h.astype(BF16))
        o0 = o0 + _dot(a_rb, sa0_h.astype(BF16)) + _dot(a_rk, vh)
        yield
    bd = be * dec
    kd = kt * dec
    outs = []
    for c in range(TS_D // CHUNK):
        rows = slice(c * CHUNK, (c + 1) * CHUNK)
        st = st_ref[...]
        stb = st.astype(BF16)
        sa_c = sa0[rows] + _dot_nt(qt[rows].astype(BF16), stb)
        outs.append(o0[rows] + _dot_nt(qo[rows].astype(BF16), stb))
        lhs_t = jnp.concatenate([sa_c, vv[rows]], axis=0).T.astype(BF16)
        rhs = jnp.concatenate([bd[rows], kd[rows]], axis=0).astype(BF16)
        upd = _dot(lhs_t, rhs)
        st_ref[...] = st * jnp.exp(bl[c * CHUNK:c * CHUNK + 1, :]) + jnp.where(same, upd, 0.0)
        yield
    o = jnp.concatenate(outs, axis=0)
    mean = group_sum(o) * (1.0 / HEAD_DIM)
    cen = o - mean
    var = group_sum(cen * cen) * (1.0 / HEAD_DIM)
    o = cen * lax.rsqrt(var + GN_EPS) * gg_ref[...] + gb_ref[...]
    o_ref[...] = o + group_sum(rr * kt * rk_ref[...]) * vv


def _rwkv7(ph, mu, w0, w2, a0, kkk, ka, rk, gg, gb, b, s):
    nt = s // TS_D
    nb = NB_D if b % NB_D == 0 else 1
    c = H_D // WIDTH
    row = lambda w, col: pl.BlockSpec((nb, TS_D, w), lambda bb, i: (bb, i, col))
    vec = lambda w: pl.BlockSpec((1, w), lambda bb, i: (0, 0))
    ph3 = ph.reshape(b, s, NPH)
    mur, muk, muv, muwa = mu[:, 0:256], mu[:, 256:512], mu[:, 512:768], mu[:, 768:896]
    carry = lambda w: pltpu.VMEM((nb, 8, w), F32)
    out = pl.pallas_call(
        _rwkv_kernel,
        grid=(b // nb, nt),
        in_specs=[row(WIDTH, c), row(WIDTH, c + 1), row(WIDTH, c + 2), row(128, H_DWA // 128),
                  vec(WIDTH), vec(WIDTH), vec(WIDTH), vec(128), vec(WIDTH),
                  pl.BlockSpec((128, 2 * WIDTH), lambda bb, i: (0, 0)), vec(WIDTH), vec(WIDTH), vec(WIDTH),
                  vec(WIDTH), vec(WIDTH), vec(WIDTH)],
        out_specs=row(WIDTH, 0),
        out_shape=jax.ShapeDtypeStruct((b, s, WIDTH), F32),
        scratch_shapes=[pltpu.VMEM((nb, WIDTH, WIDTH), F32), carry(WIDTH), carry(WIDTH), carry(WIDTH), carry(128)],
        compiler_params=_cparams(("parallel", "arbitrary")),
        name="rwkv7",
    )(ph3, ph3, ph3, ph3, mur, muk, muv, muwa, w0, w2, a0, kkk, ka, rk, gg, gb)
    return out.reshape(b * s, WIDTH)


TM_M = 512


def _merge_kernel(oa_ref, ob_ref, oc_ref, od_ref, za_ref, zb_ref, zc_ref, zd_ref,
                  ga_ref, gb_ref, gc_ref, gd_ref, x_ref, wb_ref, wo_ref, fg_ref, o_ref, *, final):
    def branch(o_r, z_r, g_r, idx):
        z = z_r[...].astype(F32)
        u = (o_r[...] * (z * _sigmoid(z))).astype(BF16)
        return _sigmoid(g_r[...].astype(F32)) * _dot(u, wb_ref[idx])

    y = (branch(oa_ref, za_ref, ga_ref, 0) + branch(ob_ref, zb_ref, gb_ref, 1)
         + branch(oc_ref, zc_ref, gc_ref, 2) + branch(od_ref, zd_ref, gd_ref, 3))
    xn = x_ref[...] + _dot(y.astype(BF16), wo_ref[...])
    if final:
        inv = lax.rsqrt(jnp.mean(xn * xn, axis=-1, keepdims=True) + EPS)
        xn = (xn * inv) * fg_ref[...]
    o_ref[...] = xn


def _merge(oa, ob, oc, od, p, x2, wb, wo, fg, final):
    t = x2.shape[0]
    row = lambda w, col: pl.BlockSpec((TM_M, w), lambda i: (i, col))
    zcols = [(L_A + 3 * WIDTH) // WIDTH, (L_BQ + 2 * WIDTH) // WIDTH, (L_C + 3 * WIDTH) // WIDTH, L_DZ // WIDTH]
    return pl.pallas_call(
        functools.partial(_merge_kernel, final=final),
        grid=(t // TM_M,),
        in_specs=[row(WIDTH, 0)] * 4
        + [row(WIDTH, zc) for zc in zcols]
        + [row(D_MODEL, L_G // D_MODEL + n) for n in range(4)]
        + [row(D_MODEL, 0),
           pl.BlockSpec((4, WIDTH, D_MODEL), lambda i: (0, 0, 0)),
           pl.BlockSpec((D_MODEL, D_MODEL), lambda i: (0, 0)),
           pl.BlockSpec((1, D_MODEL), lambda i: (0, 0))],
        out_specs=row(D_MODEL, 0),
        out_shape=jax.ShapeDtypeStruct((t, D_MODEL), F32),
        compiler_params=_cparams(("parallel",)),
        name="merge",
    )(oa, ob, oc, od, p, p, p, p, p, p, p, p, x2, wb, wo, fg)


def _split_w_in(w):
    def seg(lo, hi, width):
        return jnp.pad(w[:, lo:hi], ((0, 0), (0, width - (hi - lo))))
    c_idx, c_end = 2048 + 4 * WIDTH, 2048 + 1608
    d_end = c_end + 896
    b_f = 1024 + WIDTH
    w_hi = jnp.concatenate([seg(c_idx, c_end, H_BF), w[:, b_f:b_f + WIDTH], seg(c_end, d_end, NPH - H_D)], axis=1)
    w_lo = jnp.concatenate([w[:, :b_f], w[:, b_f + WIDTH:c_idx], w[:, d_end:]], axis=1)
    return w_hi.astype(BF16), w_lo.astype(BF16)


def kernel(x, norm_g, w_in, a_rel_bias, b_lb_logits, b_norm_g, d_mu, d_w0, d_w_up, d_a0, d_a_up,
           d_k_k, d_k_a, d_r_k, d_gn_g, d_gn_b, w_branch, w_out, final_g):
    b, s, d = x.shape
    depth = norm_g.shape[0]
    t = b * s
    x2 = x.reshape(t, d)
    cos, sin = _rope_tables(s)
    lb_all = jnp.cumsum(jax.nn.softmax(b_lb_logits.astype(F32), axis=0), axis=0)
    lb_all = lb_all - lb_all[0:1]
    tm_in = next(m for m in (1024, 512, 256) if t % m == 0)
    for l in range(depth):
        w_hi, w_lo = _split_w_in(w_in[l])
        ph = _inproj(x2, norm_g[l][None, :], w_hi, tm_in, NPH, F32)
        p16 = _inproj(x2, norm_g[l][None, :], w_lo, tm_in, NPL // 2, BF16)
        oa = _band_attention(p16, _band_bias(a_rel_bias[l]), b, s)
        lb = lb_all[l][None, :]
        ob = _hgrn2(ph, p16, jnp.log(lb), jnp.log1p(-lb), 1.0 - lb, b_norm_g[l][None, :], b, s)
        qr, kr, vt, qir, kir = _rope_prep(ph, p16, cos, sin, b, s)
        oc = _dsa(ph, qr, kr, vt, qir, kir, b, s)
        zeros = jnp.zeros((LORA, WIDTH), F32)
        w2 = jnp.concatenate([jnp.concatenate([d_w_up[l], zeros], axis=1),
                              jnp.concatenate([zeros, d_a_up[l]], axis=1)], axis=0).astype(BF16)
        od = _rwkv7(ph, d_mu[l][None, :], d_w0[l][None, :], w2, d_a0[l][None, :], d_k_k[l][None, :], d_k_a[l][None, :],
                    d_r_k[l].reshape(1, WIDTH), d_gn_g[l][None, :], d_gn_b[l][None, :], b, s)
        x2 = _merge(oa, ob, oc, od, p16, x2, w_branch[l].astype(BF16), w_out[l].astype(BF16),
                    final_g[None, :], l == depth - 1)
    return x2.reshape(b, s, d)
```

```python
import functools
import itertools
import math

import numpy as np
import jax
import jax.numpy as jnp
from jax import lax
from jax.experimental import pallas as pl
from jax.experimental.pallas import tpu as pltpu

F32 = jnp.float32
BF16 = jnp.bfloat16

D_MODEL = 1024
HEADS = 4
HEAD_DIM = 64
WIDTH = HEADS * HEAD_DIM
CHUNK = 64
A_LEFT_CHUNKS = 8
REL_CLIP = 128
IDX_HEADS = 8
IDX_DIM = 64
TOPK_MAX = 256
LORA = 64
ROPE_THETA = 10000.0
EPS = 1e-6
GN_EPS = 64e-5
NEG = -1e30
INT_MIN = -2 ** 31
I16 = jnp.int16
I16_MIN = -2 ** 15
COARSE = jnp.bfloat16
KEY_NEG_INF = INT_MIN + 0x7FFFFF

H_CQI = 0
H_CKW = 512
H_BF = 768
H_D = 1024
H_DWA = 1792
NPH = 2048
L_A = 0
L_BQ = 1024
L_C = 1792
L_DZ = 2816
L_G = 3072
NPL = 7168
VMEM_LIMIT = 56 * 1024 * 1024

_NT = (((1,), (1,)), ((), ()))


def _sigmoid(x):
    return 1.0 / (1.0 + jnp.exp(-x))


def _dot(a, b, precision=None):
    return jnp.dot(a, b, preferred_element_type=F32, precision=precision)


def _dot_nt(a, b, precision=None):
    return lax.dot_general(a, b, _NT, preferred_element_type=F32, precision=precision)


def _dot3(a, b, left):
    x = a if left else b
    hi = x.astype(BF16)
    rest = x - hi.astype(F32)
    mid = rest.astype(BF16)
    lo = (rest - mid.astype(F32)).astype(BF16)
    if left:
        return _dot(hi, b) + _dot(mid, b) + _dot(lo, b)
    return _dot(a, hi) + _dot(a, mid) + _dot(a, lo)


def _cparams(sem):
    return pltpu.CompilerParams(dimension_semantics=sem, vmem_limit_bytes=VMEM_LIMIT)


def _inproj_kernel(x_ref, g_ref, w_ref, o_ref, h_ref):
    @pl.when(pl.program_id(1) == 0)
    def _():
        x = x_ref[...]
        inv = lax.rsqrt(jnp.mean(x * x, axis=-1, keepdims=True) + EPS)
        h_ref[...] = ((x * inv) * g_ref[...]).astype(BF16)

    o_ref[...] = _dot(h_ref[...], w_ref[...]).astype(o_ref.dtype)


def _inproj(x2, g, w, tm, tn, dtype):
    t = x2.shape[0]
    n = w.shape[1]
    return pl.pallas_call(
        _inproj_kernel,
        grid=(t // tm, n // tn),
        in_specs=[
            pl.BlockSpec((tm, D_MODEL), lambda i, j: (i, 0)),
            pl.BlockSpec((1, D_MODEL), lambda i, j: (0, 0)),
            pl.BlockSpec((D_MODEL, tn), lambda i, j: (0, j)),
        ],
        out_specs=pl.BlockSpec((tm, tn), lambda i, j: (i, j)),
        out_shape=jax.ShapeDtypeStruct((t, n), dtype),
        scratch_shapes=[pltpu.VMEM((tm, D_MODEL), BF16)],
        compiler_params=_cparams(("parallel", "arbitrary")),
        name="inproj",
    )(x2, g, w)


TQ_A = 256
WIN_A = 3 * TQ_A


def _band_kernel(q_ref, k0_ref, k1_ref, k2_ref, v0_ref, v1_ref, v2_ref, bias_ref, o_ref):
    i = pl.program_id(1)
    q = q_ref[...]
    kc = jnp.concatenate([k0_ref[...], k1_ref[...], k2_ref[...]], axis=0).astype(BF16)
    vc = jnp.concatenate([v0_ref[...], v1_ref[...], v2_ref[...]], axis=0).astype(BF16)
    lane = lax.broadcasted_iota(jnp.int32, (1, WIDTH), 1)
    kpos = lax.broadcasted_iota(jnp.int32, (1, WIN_A), 1) + (i - 2) * TQ_A
    kvalid = kpos >= 0
    q4 = jnp.concatenate([jnp.where((lane // HEAD_DIM) == h, q, jnp.zeros((), q.dtype)) for h in range(HEADS)],
                         axis=0).astype(BF16)
    s = _dot_nt(q4, kc).reshape(HEADS, TQ_A, WIN_A) * (HEAD_DIM ** -0.5) + bias_ref[...]
    s = jnp.where(kvalid, s, NEG)
    m = jnp.max(s, axis=-1, keepdims=True)
    p = jnp.exp(s - m)
    l = jnp.sum(p, axis=-1, keepdims=True)
    o = _dot(p.reshape(HEADS * TQ_A, WIN_A).astype(BF16), vc).reshape(HEADS, TQ_A, WIDTH) / l
    out = jnp.zeros((TQ_A, WIDTH), F32)
    for h in range(HEADS):
        out = jnp.where((lane // HEAD_DIM) == h, o[h], out)
    o_ref[...] = out


def _band_attention(p, bias, b, s):
    nt = s // TQ_A
    qcol, kcol, vcol = L_A // WIDTH, L_A // WIDTH + 1, L_A // WIDTH + 2

    def kv_spec(col, back):
        return pl.BlockSpec((TQ_A, WIDTH), lambda bb, i: (bb * nt + jnp.maximum(i - back, 0), col))

    return pl.pallas_call(
        _band_kernel,
        grid=(b, nt),
        in_specs=[
            pl.BlockSpec((TQ_A, WIDTH), lambda bb, i: (bb * nt + i, qcol)),
            kv_spec(kcol, 2), kv_spec(kcol, 1), kv_spec(kcol, 0),
            kv_spec(vcol, 2), kv_spec(vcol, 1), kv_spec(vcol, 0),
            pl.BlockSpec((HEADS, TQ_A, WIN_A), lambda bb, i: (0, 0, 0)),
        ],
        out_specs=pl.BlockSpec((TQ_A, WIDTH), lambda bb, i: (bb * nt + i, 0)),
        out_shape=jax.ShapeDtypeStruct((b * s, WIDTH), F32),
        compiler_params=_cparams(("parallel", "parallel")),
        name="band_attention",
    )(p, p, p, p, p, p, p, bias)


def _band_bias(rel_bias):
    tab = rel_bias.astype(F32)
    n = tab.shape[1]
    d_lo, d_hi = 2 * TQ_A - (WIN_A - 1), 2 * TQ_A + TQ_A - 1
    n_lo = -(CHUNK - 1) - d_lo + 1
    n_hi = d_hi - REL_CLIP + 1
    v = jnp.concatenate([jnp.repeat(tab[:, :1], n_lo, axis=1), tab[:, 1:n - 1],
                         jnp.repeat(tab[:, n - 1:], n_hi, axis=1)], axis=1)
    span = WIN_A + TQ_A
    u = jnp.concatenate([v[:, :WIN_A][:, ::-1], jnp.zeros((HEADS, span - v.shape[1]), F32),
                         v[:, WIN_A:][:, ::-1]], axis=1)
    skew = jnp.tile(u, (1, TQ_A))[:, :TQ_A * (span - 1)].reshape(HEADS, TQ_A, span - 1)[:, :, :WIN_A]
    ql = np.arange(TQ_A)[:, None]
    kl = np.arange(WIN_A)[None, :]
    lo = CHUNK * (ql // CHUNK)
    band = (kl >= lo) & (kl < lo + (A_LEFT_CHUNKS + 1) * CHUNK)
    return jnp.where(band[None], skew, NEG)


TS_B = 256
NB_B = 4
_LEVELS = (1, 2, 4, 8, 16, 32)
assert TS_B == WIDTH and CHUNK == HEAD_DIM


def _hgrn_kernel(q_ref, f_ref, i_ref, la_ref, l1_ref, oml_ref, ng_ref, o_ref, st_ref):
    @pl.when(pl.program_id(1) == 0)
    def _():
        st_ref[...] = jnp.zeros(st_ref.shape, F32)

    steps = [_hgrn_step(q_ref.at[bi], f_ref.at[bi], i_ref.at[bi], la_ref, l1_ref, oml_ref, ng_ref,
                        o_ref.at[bi], st_ref.at[bi]) for bi in range(q_ref.shape[0])]
    for _ in itertools.zip_longest(*steps):
        pass


def _hgrn_step(q_ref, f_ref, i_ref, la_ref, l1_ref, oml_ref, ng_ref, o_ref, st_ref):
    fl = f_ref[...]
    logsig = jnp.minimum(fl, 0.0) - jnp.log1p(jnp.exp(-jnp.abs(fl)))
    y = l1_ref[...] + logsig
    la = la_ref[...]
    g = jnp.maximum(la, y) + jnp.log1p(jnp.exp(-jnp.abs(la - y)))
    kk = oml_ref[...] * _sigmoid(-fl)
    qq = q_ref[...].astype(F32)
    qf = qq * _sigmoid(qq)
    vv = i_ref[...].astype(F32)
    tt = lax.broadcasted_iota(jnp.int32, (TS_B, TS_B), 0)
    ss = lax.broadcasted_iota(jnp.int32, (TS_B, TS_B), 1)
    same = (tt // CHUNK) == (ss // CHUNK)
    ones_bd = jnp.where(same, 1.0, 0.0).astype(BF16)
    ltri_bd = jnp.where(same & (tt >= ss), 1.0, 0.0).astype(BF16)
    hid = lax.broadcasted_iota(jnp.int32, (1, WIDTH), 1) // HEAD_DIM
    heads4 = lambda x: jnp.concatenate([jnp.where(hid == h, x, 0.0) for h in range(HEADS)], axis=0).astype(BF16)
    tile4 = lambda mask: jnp.concatenate([mask] * HEADS, axis=0)

    b = _dot3(ltri_bd, g, left=False)
    bl = _dot3(ones_bd, g, left=False)
    att4 = jnp.where(tile4(tt == ss), _dot_nt(heads4(qf), kk.astype(BF16)), 0.0)
    for m in _LEVELS:
        pick = jnp.where(ss == (tt // (2 * m)) * (2 * m) + (m - 1), 1.0, 0.0).astype(BF16)
        b_hi = b.astype(BF16)
        br = _dot(pick, b_hi) + _dot(pick, (b - b_hi.astype(F32)).astype(BF16))
        ql = qf * jnp.exp(jnp.minimum(b - br, 1.0))
        kl = kk * jnp.exp(jnp.minimum(br - b, 1.0))
        mask = ((tt // (2 * m)) == (ss // (2 * m))) & ((tt % (2 * m)) >= m) & ((ss % (2 * m)) < m)
        att4 = att4 + jnp.where(tile4(mask), _dot_nt(heads4(ql), kl.astype(BF16)), 0.0)
        yield
    att4 = att4.astype(BF16)
    o = jnp.zeros((TS_B, WIDTH), F32)
    for h in range(HEADS):
        o = o + _dot(att4[h * TS_B:(h + 1) * TS_B], jnp.where(hid == h, vv, 0.0).astype(BF16))
        yield
    qe = (qf * jnp.exp(b)).astype(BF16)
    kd = (kk * jnp.exp(bl - b)).astype(BF16)
    outs = []
    for c in range(TS_B // CHUNK):
        rows = slice(c * CHUNK, (c + 1) * CHUNK)
        st = st_ref[...]
        outs.append(_dot_nt(qe[rows], st.astype(BF16)))
        upd = _dot(vv[rows].T.astype(BF16), kd[rows])
        st_ref[...] = st * jnp.exp(bl[c * CHUNK:c * CHUNK + 1, :]) + jnp.where(same, upd, 0.0)
        yield
    o = o + jnp.concatenate(outs, axis=0)
    ms = _dot3(o * o, ones_bd, left=True) * (1.0 / HEAD_DIM)
    o_ref[...] = o * lax.rsqrt(ms + EPS) * ng_ref[...]


def _hgrn2(ph, pl16, la, l1, oml, ng, b, s):
    nt = s // TS_B
    nb = NB_B if b % NB_B == 0 else 1
    row = lambda col: pl.BlockSpec((nb, TS_B, WIDTH), lambda bb, i: (bb, i, col))
    vec = pl.BlockSpec((1, WIDTH), lambda bb, i: (0, 0))
    ph3 = ph.reshape(b, s, NPH)
    pl3 = pl16.reshape(b, s, NPL)
    out = pl.pallas_call(
        _hgrn_kernel,
        grid=(b // nb, nt),
        in_specs=[row(L_BQ // WIDTH), row(H_BF // WIDTH), row(L_BQ // WIDTH + 1), vec, vec, vec, vec],
        out_specs=row(0),
        out_shape=jax.ShapeDtypeStruct((b, s, WIDTH), F32),
        scratch_shapes=[pltpu.VMEM((nb, WIDTH, WIDTH), F32)],
        compiler_params=_cparams(("parallel", "arbitrary")),
        name="hgrn2",
    )(pl3, ph3, pl3, la, l1, oml, jnp.tile(ng, (1, HEADS)))
    return out.reshape(b * s, WIDTH)


TQ_C = 512
KB_C = 512
TM_R = KB_C


def _rope(x, cos, sin_signed):
    w = x.shape[-1]
    lane = lax.broadcasted_iota(jnp.int32, x.shape, 1)
    low = (lane % HEAD_DIM) < (HEAD_DIM // 2)
    swapped = jnp.where(low, pltpu.roll(x, w - HEAD_DIM // 2, 1), pltpu.roll(x, HEAD_DIM // 2, 1))
    return x * cos + swapped * sin_signed


def _rope_kernel(q_ref, k_ref, v_ref, qi_ref, kw_ref, cos_ref, sin_ref, qo_ref, ko_ref, vo_ref, qio_ref, kio_ref):
    cos = cos_ref[...]
    sin = sin_ref[...]
    qo_ref[...] = _rope(q_ref[...].astype(F32), cos, sin).astype(BF16)
    ko_ref[...] = _rope(k_ref[...].astype(F32), cos, sin).astype(BF16)
    vo_ref[0] = v_ref[...].astype(F32).T.astype(BF16)
    cos2 = jnp.concatenate([cos, cos], axis=1)
    sin2 = jnp.concatenate([sin, sin], axis=1)
    qio_ref[...] = _rope(qi_ref[...], cos2, sin2).astype(BF16)
    kr = _rope(kw_ref[...], cos[:, :128], sin[:, :128])
    lane = lax.broadcasted_iota(jnp.int32, kr.shape, 1)
    kio_ref[...] = jnp.where(lane < IDX_DIM, kr, pltpu.roll(kr, IDX_DIM, 1)).astype(BF16)


def _rope_prep(ph, pl16, cos, sin, b, s):
    t = b * s
    nt = s // TM_R
    c = L_C // WIDTH
    row = lambda w, col: pl.BlockSpec((TM_R, w), lambda i: (i, col))
    tab = pl.BlockSpec((TM_R, WIDTH), lambda i: (i % nt, 0))
    return pl.pallas_call(
        _rope_kernel,
        grid=(t // TM_R,),
        in_specs=[row(WIDTH, c), row(WIDTH, c + 1), row(WIDTH, c + 2),
                  row(512, H_CQI // 512), row(128, H_CKW // 128), tab, tab],
        out_specs=[row(WIDTH, 0), row(WIDTH, 0), pl.BlockSpec((1, WIDTH, TM_R), lambda i: (i, 0, 0)),
                   row(512, 0), row(128, 0)],
        out_shape=[jax.ShapeDtypeStruct((t, WIDTH), BF16)] * 2
        + [jax.ShapeDtypeStruct((t // TM_R, WIDTH, TM_R), BF16),
           jax.ShapeDtypeStruct((t, 512), BF16), jax.ShapeDtypeStruct((t, 128), BF16)],
        compiler_params=_cparams(("parallel",)),
        name="rope_prep",
    )(pl16, pl16, pl16, ph, ph, cos, sin)


def _rope_tables(s):
    half = HEAD_DIM // 2
    freqs = ROPE_THETA ** (-jnp.arange(half, dtype=F32) / half)
    ang = jnp.arange(s, dtype=jnp.int32).astype(F32)[:, None] * freqs[None, :]
    cos = jnp.cos(ang)
    sin = jnp.sin(ang)
    cos = jnp.tile(jnp.concatenate([cos, cos], axis=1), (1, HEADS))
    sin = jnp.tile(jnp.concatenate([-sin, sin], axis=1), (1, HEADS))
    return cos, sin


def _dsa_kernel(q_ref, k_ref, vt_ref, qi_ref, ki_ref, w_ref, o_ref, xb_ref, sf_ref, *, topk, idx_bits):
    i = pl.program_id(1)
    nkb = ((i + 1) * TQ_C + KB_C - 1) // KB_C
    lane128 = lax.broadcasted_iota(jnp.int32, (1, 128), 1)
    lane256 = lax.broadcasted_iota(jnp.int32, (1, WIDTH), 1)
    krow = lax.broadcasted_iota(jnp.int32, (KB_C, 1), 0)
    qchunk = (i * TQ_C + lax.broadcasted_iota(jnp.int32, (1, TQ_C), 1)) // CHUNK
    wt = (w_ref[...] * (IDX_HEADS ** -0.5)).T
    qi = qi_ref[...] * jnp.asarray(IDX_DIM ** -0.5, BF16)
    qi_heads = [jnp.where((lane128 // IDX_DIM) == (h % 2), qi[:, 128 * (h // 2):128 * (h // 2 + 1)],
                          jnp.zeros((), BF16)) for h in range(IDX_HEADS)]

    def score_block(kb, carry):
        off = pl.multiple_of(kb * KB_C, KB_C)
        kib = ki_ref[pl.ds(off, KB_C), :]
        acc = jnp.zeros((KB_C, TQ_C), F32)
        for h in range(IDX_HEADS):
            acc = acc + jnp.maximum(_dot_nt(kib, qi_heads[h]), 0.0) * wt[IDX_DIM + h:IDX_DIM + h + 1, :]
        adm = ((off + krow) // CHUNK) <= qchunk
        sc = jnp.where(adm, acc, -jnp.inf)
        sf_ref[kb] = sc
        xb_ref[kb] = sc.astype(COARSE)
        return carry

    lax.fori_loop(0, nkb, score_block, 0)

    def float_of_key(key):
        key = jnp.maximum(key, KEY_NEG_INF)
        return lax.bitcast_convert_type(key ^ ((key >> 31) & jnp.int32(0x7FFFFFFF)), F32)

    def coarse_of_key(key16):
        return float_of_key(jnp.where(key16 < 0, (key16 << 16) + 0xFFFF, key16 << 16))

    one16 = jnp.ones((), I16)
    zero16 = jnp.zeros((), I16)

    def count16(pred):
        def body(kb, acc):
            c3 = jnp.where(pred(xb_ref[kb]), one16, zero16).reshape(KB_C // 16, 16, TQ_C)
            return acc + functools.reduce(lambda a, b: a + b, [c3[g] for g in range(KB_C // 16)])
        acc = lax.fori_loop(0, nkb, body, jnp.zeros((16, TQ_C), I16))
        return jnp.sum(acc.astype(F32), axis=0, keepdims=True)

    def count32(pred):
        def body(kb, acc):
            c3 = jnp.where(pred(kb, sf_ref[kb]), 1.0, 0.0).reshape(KB_C // 8, 8, TQ_C)
            return acc + functools.reduce(lambda a, b: a + b, [c3[g] for g in range(KB_C // 8)])
        acc = lax.fori_loop(0, nkb, body, jnp.zeros((8, TQ_C), F32))
        return jnp.sum(acc, axis=0, keepdims=True)

    def coarse_bit(bi, t):
        cand = t + lax.shift_left(jnp.int32(1), 15 - bi)
        cb = coarse_of_key(cand).astype(COARSE)
        return jnp.where(count16(lambda x: x >= cb) >= topk, cand, t)

    t16 = lax.fori_loop(0, 16, coarse_bit, jnp.full((1, TQ_C), I16_MIN, jnp.int32))
    tau_c = coarse_of_key(t16)
    next_c = coarse_of_key(t16 + 1)
    tau_cb = tau_c.astype(COARSE)
    next_cb = next_c.astype(COARSE)
    real = tau_c > -jnp.inf
    above_cb = jnp.where(real, next_c, float(jnp.finfo(COARSE).min)).astype(COARSE)
    kth_fine = topk - count16(lambda x: x >= next_cb)

    def mask_bucket(kb, carry):
        x = xb_ref[kb]
        inside = jnp.where((x >= tau_cb) & jnp.logical_not(x >= next_cb), one16, zero16).astype(F32)
        sf_ref[kb] = jnp.where(inside > 0.5, sf_ref[kb], -jnp.inf)
        return carry

    lax.fori_loop(0, nkb, mask_bucket, 0)

    t16s = jnp.where(t16 < 0, (t16 << 16) + 0xFFFF, t16 << 16)

    def fine_bit(bi, t):
        cand = t + lax.shift_left(jnp.int32(1), 16 - bi)
        cf = float_of_key(cand)
        return jnp.where(count32(lambda kb, sc: sc >= cf) >= kth_fine, cand, t)

    tau = float_of_key(lax.fori_loop(0, 17, fine_bit, t16s - 0x8000))
    need = kth_fine - count32(lambda kb, sc: sc > tau)
    ties = count32(lambda kb, sc: sc == tau)

    def index_search():
        def index_bit(bi, jmax):
            cand = jmax + lax.shift_left(jnp.int32(1), idx_bits - 1 - bi)
            cnt = count32(lambda kb, sc: (sc == tau) & ((kb * KB_C + krow) < cand))
            return jnp.where(cnt < need, cand, jmax)
        return lax.fori_loop(0, idx_bits, index_bit, jnp.zeros((1, TQ_C), jnp.int32))

    excess = jnp.max(jnp.where((ties > need) & real, 1.0, 0.0)) > 0.5
    jmax = lax.cond(excess, index_search, lambda: jnp.full((1, TQ_C), 2 ** (idx_bits - 1), jnp.int32))

    q = q_ref[...] * jnp.asarray(HEAD_DIM ** -0.5, BF16)
    q4 = jnp.concatenate([jnp.where((lane256 // HEAD_DIM) == h, q, jnp.zeros((), BF16)) for h in range(HEADS)],
                         axis=0)
    half = HEADS * TQ_C // 2

    def attend(blocks, carry):
        m_old, l_old, acc_old = carry
        scores = []
        for kb in blocks:
            off = pl.multiple_of(kb * KB_C, KB_C)
            kblk = k_ref[pl.ds(off, KB_C), :]
            x = xb_ref[kb]
            sc = sf_ref[kb]
            sel = (jnp.where(x >= above_cb, one16, zero16).astype(F32) > 0.5) | (
                real & ((sc > tau) | ((sc == tau) & ((off + krow) <= jmax))))
            madd = jnp.where(sel, 0.0, NEG)
            s = jnp.concatenate([_dot_nt(kblk, q4[:half]), _dot_nt(kblk, q4[half:])], axis=1)
            scores.append(s + jnp.concatenate([madd] * HEADS, axis=1))
        m_new = m_old
        for s in scores:
            m_new = jnp.maximum(m_new, jnp.max(s, axis=0, keepdims=True))
        corr = jnp.exp(m_old - m_new)
        l_new = corr * l_old
        acc = corr * acc_old
        for kb, s in zip(blocks, scores):
            p = jnp.exp(s - m_new)
            l_new = l_new + jnp.sum(p, axis=0, keepdims=True)
            pb = p.astype(BF16)
            vtb = vt_ref[kb]
            acc = acc + jnp.concatenate([_dot(vtb, pb[:, :half]), _dot(vtb, pb[:, half:])], axis=1)
        return m_new, l_new, acc

    cols = HEADS * TQ_C
    init = (jnp.full((1, cols), NEG, F32), jnp.zeros((1, cols), F32), jnp.zeros((WIDTH, cols), F32))
    pairs = nkb // 2
    state = lax.fori_loop(0, pairs, lambda j, c: attend([2 * j, 2 * j + 1], c), init)
    _, l_fin, acc_fin = lax.fori_loop(2 * pairs, nkb, lambda kb, c: attend([kb], c), state)
    ot = acc_fin / l_fin
    out = jnp.zeros((TQ_C, WIDTH), F32)
    for h in range(HEADS):
        out = jnp.where((lane256 // HEAD_DIM) == h, ot[:, h * TQ_C:(h + 1) * TQ_C].T, out)
    o_ref[...] = out


def _dsa(p, qr, kr, vt, qir, kir, b, s):
    nt = s // TQ_C
    nkb = s // KB_C
    topk = min(TOPK_MAX, s // 4)
    idx_bits = int(math.log2(s)) + 1
    qrow = lambda w, col: pl.BlockSpec((TQ_C, w), lambda bb, i: (bb * nt + i, col))
    full = lambda w: pl.BlockSpec((s, w), lambda bb, i: (bb, 0))
    return pl.pallas_call(
        functools.partial(_dsa_kernel, topk=topk, idx_bits=idx_bits),
        grid=(b, nt),
        in_specs=[qrow(WIDTH, 0), full(WIDTH), pl.BlockSpec((nkb, WIDTH, KB_C), lambda bb, i: (bb, 0, 0)),
                  qrow(512, 0), full(128), qrow(128, H_CKW // 128)],
        out_specs=qrow(WIDTH, 0),
        out_shape=jax.ShapeDtypeStruct((b * s, WIDTH), F32),
        scratch_shapes=[
            pltpu.VMEM((nkb, KB_C, TQ_C), COARSE),
            pltpu.VMEM((nkb, KB_C, TQ_C), F32),
        ],
        compiler_params=_cparams(("parallel", "arbitrary")),
        name="dsa",
    )(qr, kr, vt, qir, kir, p)


TS_D = 256
NB_D = 4
assert TS_D == WIDTH and CHUNK == HEAD_DIM


def _rwkv_kernel(r_ref, k_ref, v_ref, wa_ref, mur_ref, muk_ref, muv_ref, muwa_ref, w0_ref, w2_ref, a0_ref,
                 kkk_ref, ka_ref, rk_ref, gg_ref, gb_ref, o_ref, st_ref, pr_ref, pk_ref, pv_ref, pwa_ref):
    @pl.when(pl.program_id(1) == 0)
    def _():
        for ref in (st_ref, pr_ref, pk_ref, pv_ref, pwa_ref):
            ref[...] = jnp.zeros(ref.shape, F32)

    params = (mur_ref, muk_ref, muv_ref, muwa_ref, w0_ref, w2_ref, a0_ref, kkk_ref, ka_ref, rk_ref, gg_ref, gb_ref)
    steps = [_rwkv_step(*(x.at[bi] for x in (r_ref, k_ref, v_ref, wa_ref)), *params,
                        *(x.at[bi] for x in (o_ref, st_ref, pr_ref, pk_ref, pv_ref, pwa_ref)))
             for bi in range(r_ref.shape[0])]
    for _ in itertools.zip_longest(*steps):
        pass


def _rwkv_step(r_ref, k_ref, v_ref, wa_ref, mur_ref, muk_ref, muv_ref, muwa_ref, w0_ref, w2_ref, a0_ref,
               kkk_ref, ka_ref, rk_ref, gg_ref, gb_ref, o_ref, st_ref, pr_ref, pk_ref, pv_ref, pwa_ref):
    def shift(x_ref, p_ref, mu_ref):
        x = x_ref[...]
        rid = lax.broadcasted_iota(jnp.int32, x.shape, 0)
        prev = jnp.where(rid == 0, p_ref[0:1, :], pltpu.roll(x, 1, 0))
        p_ref[0:1, :] = x[TS_D - 1:TS_D, :]
        return x + mu_ref[...] * (prev - x)

    rr = shift(r_ref, pr_ref, mur_ref)
    kf = shift(k_ref, pk_ref, muk_ref)
    vv = shift(v_ref, pv_ref, muv_ref)
    wa = shift(wa_ref, pwa_ref, muwa_ref)
    lane_wa = lax.broadcasted_iota(jnp.int32, wa.shape, 1)
    lora = _dot(jnp.where(lane_wa < LORA, jnp.tanh(wa), wa).astype(BF16), w2_ref[...])
    zz = w0_ref[...] + lora[:, :WIDTH]
    lw = -jnp.exp(-(jnp.maximum(-zz, 0.0) + jnp.log1p(jnp.exp(-jnp.abs(zz)))) - 0.5)
    aa = _sigmoid(a0_ref[...] + lora[:, WIDTH:])
    kt = kf * (1.0 + (aa - 1.0) * ka_ref[...])
    kkr = kf * kkk_ref[...]
    yield
    tt = lax.broadcasted_iota(jnp.int32, (TS_D, TS_D), 0)
    ss = lax.broadcasted_iota(jnp.int32, (TS_D, TS_D), 1)
    same = (tt // CHUNK) == (ss // CHUNK)
    strict = same & (tt > ss)
    incl = same & (tt >= ss)
    eye = jnp.where(tt == ss, 1.0, 0.0)
    ones_bd = jnp.where(same, 1.0, 0.0).astype(BF16)
    ltri_bd = jnp.where(incl, 1.0, 0.0).astype(BF16)
    hid = lax.broadcasted_iota(jnp.int32, (1, WIDTH), 1) // HEAD_DIM

    def group_sum(x):
        return _dot3(x, ones_bd, left=True)

    kk = kkr / jnp.maximum(jnp.sqrt(group_sum(kkr * kkr)), 1e-12)
    be = kk * aa
    b = _dot3(ltri_bd, lw, left=False)
    bl = _dot3(ones_bd, lw, left=False)
    eb = jnp.exp(b)
    enb = jnp.exp(-b)
    dec = jnp.exp(bl - b)
    qa = -kk * jnp.exp(b - lw)
    qr = rr * eb
    q2 = jnp.concatenate([qa, qr], axis=0)
    k2 = jnp.concatenate([be * enb, kt * enb], axis=0).astype(BF16)
    zero = jnp.zeros((TS_D, WIDTH), F32)
    sa0, qt, qo, o0 = zero, zero, zero, zero
    for h in range(HEADS):
        hm = hid == h
        g = _dot_nt(jnp.where(hm, q2, 0.0).astype(BF16), k2)
        a_ab = jnp.where(strict, g[:TS_D, :TS_D], 0.0)
        a_ak = jnp.where(strict, g[:TS_D, TS_D:], 0.0).astype(BF16)
        a_rb = jnp.where(incl, g[TS_D:, :TS_D], 0.0).astype(BF16)
        a_rk = jnp.where(incl, g[TS_D:, TS_D:], 0.0).astype(BF16)
        tm = eye + a_ab
        pw = a_ab.astype(BF16)
        for _ in range(int(math.log2(CHUNK)) - 1):
            pwf = _dot(pw, pw)
            yield
            pw = pwf.astype(BF16)
            tm = tm + _dot(tm.astype(BF16), pw)
            yield
        tmb = tm.astype(BF16)
        vh = jnp.where(hm, vv, 0.0).astype(BF16)
        sa0_h = _dot(tmb, _dot(a_ak, vh).astype(BF16))
        yield
        qt_h = _dot(tmb, jnp.where(hm, qa, 0.0).astype(BF16))
        yield
        sa0 = sa0 + sa0_h
        qt = qt + qt_h
        qo = qo + jnp.where(hm, qr, 0.0) + _dot(a_rb, qt_h.astype(BF16))
        o0 = o0 + _dot(a_rb, sa0_h.astype(BF16)) + _dot(a_rk, vh)
        yield
    bd = be * dec
    kd = kt * dec
    outs = []
    for c in range(TS_D // CHUNK):
        rows = slice(c * CHUNK, (c + 1) * CHUNK)
        st = st_ref[...]
        stb = st.astype(BF16)
        sa_c = sa0[rows] + _dot_nt(qt[rows].astype(BF16), stb)
        outs.append(o0[rows] + _dot_nt(qo[rows].astype(BF16), stb))
        lhs_t = jnp.concatenate([sa_c, vv[rows]], axis=0).T.astype(BF16)
        rhs = jnp.concatenate([bd[rows], kd[rows]], axis=0).astype(BF16)
        upd = _dot(lhs_t, rhs)
        st_ref[...] = st * jnp.exp(bl[c * CHUNK:c * CHUNK + 1, :]) + jnp.where(same, upd, 0.0)
        yield
    o = jnp.concatenate(outs, axis=0)
    mean = group_sum(o) * (1.0 / HEAD_DIM)
    cen = o - mean
    var = group_sum(cen * cen) * (1.0 / HEAD_DIM)
    o = cen * lax.rsqrt(var + GN_EPS) * gg_ref[...] + gb_ref[...]
    o_ref[...] = o + group_sum(rr * kt * rk_ref[...]) * vv


def _rwkv7(ph, mu, w0, w2, a0, kkk, ka, rk, gg, gb, b, s):
    nt = s // TS_D
    nb = NB_D if b % NB_D == 0 else 1
    c = H_D // WIDTH
    row = lambda w, col: pl.BlockSpec((nb, TS_D, w), lambda bb, i: (bb, i, col))
    vec = lambda w: pl.BlockSpec((1, w), lambda bb, i: (0, 0))
    ph3 = ph.reshape(b, s, NPH)
    mur, muk, muv, muwa = mu[:, 0:256], mu[:, 256:512], mu[:, 512:768], mu[:, 768:896]
    carry = lambda w: pltpu.VMEM((nb, 8, w), F32)
    out = pl.pallas_call(
        _rwkv_kernel,
        grid=(b // nb, nt),
        in_specs=[row(WIDTH, c), row(WIDTH, c + 1), row(WIDTH, c + 2), row(128, H_DWA // 128),
                  vec(WIDTH), vec(WIDTH), vec(WIDTH), vec(128), vec(WIDTH),
                  pl.BlockSpec((128, 2 * WIDTH), lambda bb, i: (0, 0)), vec(WIDTH), vec(WIDTH), vec(WIDTH),
                  vec(WIDTH), vec(WIDTH), vec(WIDTH)],
        out_specs=row(WIDTH, 0),
        out_shape=jax.ShapeDtypeStruct((b, s, WIDTH), F32),
        scratch_shapes=[pltpu.VMEM((nb, WIDTH, WIDTH), F32), carry(WIDTH), carry(WIDTH), carry(WIDTH), carry(128)],
        compiler_params=_cparams(("parallel", "arbitrary")),
        name="rwkv7",
    )(ph3, ph3, ph3, ph3, mur, muk, muv, muwa, w0, w2, a0, kkk, ka, rk, gg, gb)
    return out.reshape(b * s, WIDTH)


TM_M = 512


def _merge_kernel(oa_ref, ob_ref, oc_ref, od_ref, za_ref, zb_ref, zc_ref, zd_ref,
                  ga_ref, gb_ref, gc_ref, gd_ref, x_ref, wb_ref, wo_ref, fg_ref, o_ref, *, final):
    def branch(o_r, z_r, g_r, idx):
        z = z_r[...].astype(F32)
        u = (o_r[...] * (z * _sigmoid(z))).astype(BF16)
        return _sigmoid(g_r[...].astype(F32)) * _dot(u, wb_ref[idx])

    y = (branch(oa_ref, za_ref, ga_ref, 0) + branch(ob_ref, zb_ref, gb_ref, 1)
         + branch(oc_ref, zc_ref, gc_ref, 2) + branch(od_ref, zd_ref, gd_ref, 3))
    xn = x_ref[...] + _dot(y.astype(BF16), wo_ref[...])
    if final:
        inv = lax.rsqrt(jnp.mean(xn * xn, axis=-1, keepdims=True) + EPS)
        xn = (xn * inv) * fg_ref[...]
    o_ref[...] = xn


def _merge(oa, ob, oc, od, p, x2, wb, wo, fg, final):
    t = x2.shape[0]
    row = lambda w, col: pl.BlockSpec((TM_M, w), lambda i: (i, col))
    zcols = [(L_A + 3 * WIDTH) // WIDTH, (L_BQ + 2 * WIDTH) // WIDTH, (L_C + 3 * WIDTH) // WIDTH, L_DZ // WIDTH]
    return pl.pallas_call(
        functools.partial(_merge_kernel, final=final),
        grid=(t // TM_M,),
        in_specs=[row(WIDTH, 0)] * 4
        + [row(WIDTH, zc) for zc in zcols]
        + [row(D_MODEL, L_G // D_MODEL + n) for n in range(4)]
        + [row(D_MODEL, 0),
           pl.BlockSpec((4, WIDTH, D_MODEL), lambda i: (0, 0, 0)),
           pl.BlockSpec((D_MODEL, D_MODEL), lambda i: (0, 0)),
           pl.BlockSpec((1, D_MODEL), lambda i: (0, 0))],
        out_specs=row(D_MODEL, 0),
        out_shape=jax.ShapeDtypeStruct((t, D_MODEL), F32),
        compiler_params=_cparams(("parallel",)),
        name="merge",
    )(oa, ob, oc, od, p, p, p, p, p, p, p, p, x2, wb, wo, fg)


def _split_w_in(w):
    def seg(lo, hi, width):
        return jnp.pad(w[:, lo:hi], ((0, 0), (0, width - (hi - lo))))
    c_idx, c_end = 2048 + 4 * WIDTH, 2048 + 1608
    d_end = c_end + 896
    b_f = 1024 + WIDTH
    w_hi = jnp.concatenate([seg(c_idx, c_end, H_BF), w[:, b_f:b_f + WIDTH], seg(c_end, d_end, NPH - H_D)], axis=1)
    w_lo = jnp.concatenate([w[:, :b_f], w[:, b_f + WIDTH:c_idx], w[:, d_end:]], axis=1)
    return w_hi.astype(BF16), w_lo.astype(BF16)


def kernel(x, norm_g, w_in, a_rel_bias, b_lb_logits, b_norm_g, d_mu, d_w0, d_w_up, d_a0, d_a_up,
           d_k_k, d_k_a, d_r_k, d_gn_g, d_gn_b, w_branch, w_out, final_g):
    b, s, d = x.shape
    depth = norm_g.shape[0]
    t = b * s
    x2 = x.reshape(t, d)
    cos, sin = _rope_tables(s)
    lb_all = jnp.cumsum(jax.nn.softmax(b_lb_logits.astype(F32), axis=0), axis=0)
    lb_all = lb_all - lb_all[0:1]
    tm_in = next(m for m in (1024, 512, 256) if t % m == 0)
    for l in range(depth):
        w_hi, w_lo = _split_w_in(w_in[l])
        ph = _inproj(x2, norm_g[l][None, :], w_hi, tm_in, NPH, F32)
        p16 = _inproj(x2, norm_g[l][None, :], w_lo, tm_in, NPL // 2, BF16)
        oa = _band_attention(p16, _band_bias(a_rel_bias[l]), b, s)
        lb = lb_all[l][None, :]
        ob = _hgrn2(ph, p16, jnp.log(lb), jnp.log1p(-lb), 1.0 - lb, b_norm_g[l][None, :], b, s)
        qr, kr, vt, qir, kir = _rope_prep(ph, p16, cos, sin, b, s)
        oc = _dsa(ph, qr, kr, vt, qir, kir, b, s)
        zeros = jnp.zeros((LORA, WIDTH), F32)
        w2 = jnp.concatenate([jnp.concatenate([d_w_up[l], zeros], axis=1),
                              jnp.concatenate([zeros, d_a_up[l]], axis=1)], axis=0).astype(BF16)
        od = _rwkv7(ph, d_mu[l][None, :], d_w0[l][None, :], w2, d_a0[l][None, :], d_k_k[l][None, :], d_k_a[l][None, :],
                    d_r_k[l].reshape(1, WIDTH), d_gn_g[l][None, :], d_gn_b[l][None, :], b, s)
        x2 = _merge(oa, ob, oc, od, p16, x2, w_branch[l].astype(BF16), w_out[l].astype(BF16),
                    final_g[None, :], l == depth - 1)
    return x2.reshape(b, s, d)
```
